```python
import math
import jax, jax.numpy as jnp
from jax import lax
import numpy as np

D_MODEL = 1024
BATCH = 4
SEQ = 4096
DEPTH = 2

GRID_W = 64
CTX_LEN = 256
EPS = 1e-6
ROPE_THETA = 10000.0
Q_BLOCK = 128
CHUNK = 64

MIX_WIDTH = D_MODEL
GROUP_WIDTH = MIX_WIDTH // 4
HEAD_DIM = 64
DIFF_HEADS = GROUP_WIDTH // HEAD_DIM
DIFF_QK_DIM = HEAD_DIM // 2
GQA_HEADS = GROUP_WIDTH // HEAD_DIM
GQA_KV_HEADS = 2
MLSTM_HEADS = GROUP_WIDTH // HEAD_DIM
GDN_HEADS = GROUP_WIDTH // HEAD_DIM
GDN_CONV = 5
D_FF = 2816
N_EXPERTS = 8
TOP_K = 2
D_FF_EXPERT = 3584

SPLIT_SIZES = (
    GROUP_WIDTH, GROUP_WIDTH, GROUP_WIDTH,
    GROUP_WIDTH, GQA_KV_HEADS * HEAD_DIM, GQA_KV_HEADS * HEAD_DIM,
    GROUP_WIDTH, GROUP_WIDTH, GROUP_WIDTH, GROUP_WIDTH, 4 * MLSTM_HEADS,
    3 * GROUP_WIDTH, GROUP_WIDTH, 4 * GDN_HEADS,
)
N_IN = sum(SPLIT_SIZES)

kernel_name = 'hybrid_dit_parallel_head_groups'


def rms_norm(x, gain):
    xf = x.astype(jnp.float32)
    y = xf * lax.rsqrt(jnp.mean(xf * xf, axis=-1, keepdims=True) + EPS)
    return (y * gain.astype(jnp.float32)).astype(x.dtype)


def head_layer_norm(h, gain):
    mu = jnp.mean(h, axis=-1, keepdims=True)
    var = jnp.mean(jnp.square(h - mu), axis=-1, keepdims=True)
    return (h - mu) * lax.rsqrt(var + EPS) * gain.astype(jnp.float32).reshape(h.shape[-2:])


def l2_normalize(x):
    return x * lax.rsqrt(jnp.sum(x * x, axis=-1, keepdims=True) + EPS)


def modulate(h, shift, scale):
    return h * (1 + scale) + shift


def axial_rope_angles(n, dim):
    rows = n // GRID_W
    t = jnp.arange(rows * GRID_W)
    row = (t // GRID_W).astype(jnp.float32)
    col = (t % GRID_W).astype(jnp.float32)
    n_freq = dim // 4
    inv_freq = ROPE_THETA ** (-jnp.arange(n_freq, dtype=jnp.float32) / n_freq)
    ang = jnp.stack([row[:, None] * inv_freq, col[:, None] * inv_freq], axis=1)
    return jnp.cos(ang), jnp.sin(ang)


def apply_axial_rope(x, cos, sin):
    shp = x.shape
    xr = x.astype(jnp.float32).reshape(shp[:-1] + (2, 2, shp[-1] // 4))
    x1, x2 = xr[..., 0, :], xr[..., 1, :]
    c, s = cos[:, None], sin[:, None]
    y = jnp.stack([x1 * c - x2 * s, x2 * c + x1 * s], axis=-2)
    return y.reshape(shp).astype(x.dtype)


def softmax_f32(s):
    return jax.nn.softmax(s.astype(jnp.float32), axis=-1)


def sweep_query_blocks(core, q, q_axis, out_axis):
    n = q.shape[q_axis]
    nb = n // Q_BLOCK
    qb = q.reshape(q.shape[:q_axis] + (nb, Q_BLOCK) + q.shape[q_axis + 1:])
    out = lax.map(core, jnp.moveaxis(qb, q_axis, 0))
    out = jnp.moveaxis(out, 0, out_axis)
    return out.reshape(out.shape[:out_axis] + (n,) + out.shape[out_axis + 2:])


def diff_attn_core(q, k, v, lam):
    p = softmax_f32(jnp.einsum('bhjqd,bhjkd->bhjqk', q, k) * DIFF_QK_DIM ** -0.5)
    p = p[:, :, 0] - lam * p[:, :, 1]
    return jnp.einsum('bhqk,bhkd->bhqd', p.astype(v.dtype), v)


def gqa_core(q, k, v):
    p = softmax_f32(jnp.einsum('bhgqd,bhkd->bhgqk', q, k) * HEAD_DIM ** -0.5)
    return jnp.einsum('bhgqk,bhkd->bhgqd', p.astype(v.dtype), v)


def diff_attention(q, k, v, qc, kc, vc, qk_gain, lam_vec, subln, lambda_init, rope, with_ctx):
    H, dq = DIFF_HEADS, DIFF_QK_DIM
    maps = lambda t: t.reshape(t.shape[0], t.shape[1], 2 * H, dq)
    tqk = lambda t: t.reshape(t.shape[0], t.shape[1], H, 2, dq).transpose(0, 2, 3, 1, 4)
    tv = lambda t: t.reshape(t.shape[0], t.shape[1], H, 2 * dq).transpose(0, 2, 1, 3)
    q = apply_axial_rope(rms_norm(maps(q), qk_gain[0]), *rope)
    k = apply_axial_rope(rms_norm(maps(k), qk_gain[1]), *rope)
    kc = tqk(rms_norm(maps(kc), qk_gain[1]))
    vc = tv(vc)
    lam_vec = lam_vec.astype(jnp.float32)
    lam = jnp.exp(jnp.sum(lam_vec[0] * lam_vec[1])) - jnp.exp(jnp.sum(lam_vec[2] * lam_vec[3])) + lambda_init
    k_all = jnp.concatenate([kc, tqk(k)], axis=3)
    v_all = jnp.concatenate([vc, tv(v)], axis=2)
    o = sweep_query_blocks(lambda qb: diff_attn_core(qb, k_all, v_all, lam), tqk(q), 3, 2)

    def post(o):
        o = rms_norm(o, subln) * (1 - lambda_init)
        return o.transpose(0, 2, 1, 3).reshape(o.shape[0], o.shape[2], GROUP_WIDTH)

    if not with_ctx:
        return post(o), None
    qc = tqk(rms_norm(maps(qc), qk_gain[0]))
    return post(o), post(diff_attn_core(qc, kc, vc, lam))


def gqa_attention(q, k, v, qc, kc, vc, qk_gain, rope, with_ctx):
    G = GQA_HEADS // GQA_KV_HEADS
    heads = lambda t, h: t.reshape(t.shape[0], t.shape[1], h, HEAD_DIM)
    tq = lambda t: t.reshape(t.shape[0], t.shape[1], GQA_KV_HEADS, G, HEAD_DIM).transpose(0, 2, 3, 1, 4)
    tk = lambda t: t.transpose(0, 2, 1, 3)
    q = apply_axial_rope(rms_norm(heads(q, GQA_HEADS), qk_gain[0]), *rope)
    k = apply_axial_rope(rms_norm(heads(k, GQA_KV_HEADS), qk_gain[1]), *rope)
    kc = tk(rms_norm(heads(kc, GQA_KV_HEADS), qk_gain[1]))
    vc = tk(heads(vc, GQA_KV_HEADS))
    k_all = jnp.concatenate([kc, tk(k)], axis=2)
    v_all = jnp.concatenate([vc, tk(heads(v, GQA_KV_HEADS))], axis=2)
    o = sweep_query_blocks(lambda qb: gqa_core(qb, k_all, v_all), tq(q), 3, 3)
    post = lambda o: o.transpose(0, 3, 1, 2, 4).reshape(o.shape[0], o.shape[3], GROUP_WIDTH)
    if not with_ctx:
        return post(o), None
    qc = tq(rms_norm(heads(qc, GQA_HEADS), qk_gain[0]))
    return post(o), post(gqa_core(qc, kc, vc))


def mlstm_chunkwise(q, k, v, i_pre, f_pre, state):
    B, H, T, d = q.shape
    nc = T // CHUNK
    chunks = lambda a: jnp.moveaxis(a.reshape(a.shape[:2] + (nc, CHUNK) + a.shape[3:]), 2, 0)
    idx = jnp.arange(CHUNK)
    causal = idx[:, None] >= idx[None, :]

    def step(carry, xs):
        C, n, m = carry
        qc, kc, vc, ic, fc = xs
        b = jnp.cumsum(fc, axis=-1)
        inter = b + m[..., None]
        dmat = jnp.where(causal, b[..., :, None] - b[..., None, :] + ic[..., None, :], -jnp.inf)
        m_t = jnp.maximum(inter, jnp.max(dmat, axis=-1))
        w_inter = jnp.exp(inter - m_t)
        s = jnp.einsum('bhtd,bhsd->bhts', qc, kc) * jnp.exp(dmat - m_t[..., None])
        num = w_inter[..., None] * jnp.einsum('bhvk,bhtk->bhtv', C, qc) + jnp.einsum('bhts,bhsv->bhtv', s, vc)
        den = w_inter * jnp.einsum('bhk,bhtk->bht', n, qc) + jnp.sum(s, axis=-1)
        h = num / jnp.maximum(jnp.abs(den), jnp.exp(-m_t))[..., None]
        b_last = b[..., -1]
        g = b_last[..., None] - b + ic
        m_new = jnp.maximum(b_last + m, jnp.max(g, axis=-1))
        a_prev = jnp.exp(b_last + m - m_new)
        a_s = jnp.exp(g - m_new[..., None])
        C = a_prev[..., None, None] * C + jnp.einsum('bhs,bhsv,bhsk->bhvk', a_s, vc, kc)
        n = a_prev[..., None] * n + jnp.einsum('bhs,bhsk->bhk', a_s, kc)
        return (C, n, m_new), h

    xs = tuple(chunks(a) for a in (q, k, v, i_pre, jax.nn.log_sigmoid(f_pre)))
    state, hs = lax.scan(step, state, xs)
    return jnp.moveaxis(hs, 0, 2).reshape(B, H, T, d), state


def mlstm_mixer(lat, ctx, gate_bias, norm_gain, with_ctx):
    f32 = jnp.float32

    def prep(q, k, v, gates):
        bsz, t_len, _ = q.shape
        heads = lambda a: a.reshape(bsz, t_len, MLSTM_HEADS, HEAD_DIM).transpose(0, 2, 1, 3).astype(f32)
        gt = gates.reshape(bsz, t_len, 4, MLSTM_HEADS).astype(f32) + gate_bias.astype(f32)
        return heads(q), heads(k) * HEAD_DIM ** -0.5, heads(v), gt.transpose(2, 0, 3, 1)

    ql, kl, vl, gl = prep(lat[0], lat[1], lat[2], lat[4])
    qc, kc, vc, gc = prep(ctx[0], ctx[1], ctx[2], ctx[4])
    bsz = ql.shape[0]
    state0 = (jnp.zeros((bsz, MLSTM_HEADS, HEAD_DIM, HEAD_DIM), f32),
              jnp.zeros((bsz, MLSTM_HEADS, HEAD_DIM), f32),
              jnp.zeros((bsz, MLSTM_HEADS), f32))
    h_lat, h_ctx = [], []
    for d in range(2):
        rev = (lambda a: jnp.flip(a, axis=2)) if d == 1 else (lambda a: a)
        hc_d, st = mlstm_chunkwise(rev(qc), rev(kc), rev(vc), rev(gc[2 * d]), rev(gc[2 * d + 1]), state0)
        hl_d, _ = mlstm_chunkwise(rev(ql), rev(kl), rev(vl), rev(gl[2 * d]), rev(gl[2 * d + 1]), st)
        h_lat.append(rev(hl_d))
        h_ctx.append(rev(hc_d))

    def post(h, o):
        bsz_, t_len, _ = o.shape
        h = head_layer_norm(h.transpose(0, 2, 1, 3), norm_gain).reshape(bsz_, t_len, GROUP_WIDTH)
        return (jax.nn.sigmoid(o.astype(f32)) * h).astype(o.dtype)

    y_lat = post(h_lat[0] + h_lat[1], lat[3])
    return y_lat, (post(h_ctx[0] + h_ctx[1], ctx[3]) if with_ctx else None)


def short_conv(x, w):
    kw = w.shape[0]
    return lax.conv_general_dilated(x, w[:, None, :].astype(x.dtype), window_strides=(1,),
                                    padding=[(kw // 2, kw // 2)],
                                    dimension_numbers=('NWC', 'WIO', 'NWC'),
                                    feature_group_count=x.shape[-1])


def gated_delta_chunkwise(q, k, v, beta, g, state):
    B, H, T, dk = q.shape
    dv = v.shape[-1]
    nc = T // CHUNK
    ch = lambda a: a.reshape(a.shape[:2] + (nc, CHUNK) + a.shape[3:])
    q, k, v, beta, g = (ch(a) for a in (q, k, v, beta, g))
    gam = jnp.cumsum(g, axis=-1)
    idx = jnp.arange(CHUNK)
    causal = idx[:, None] >= idx[None, :]
    strict = idx[:, None] > idx[None, :]
    decay = jnp.exp(jnp.where(causal, gam[..., :, None] - gam[..., None, :], -jnp.inf))
    kb = k * beta[..., None]
    a = jnp.where(strict, jnp.einsum('bhcid,bhcjd->bhcij', kb, k) * decay, 0.0)
    m = a + jnp.eye(CHUNK, dtype=a.dtype)
    rhs = jnp.concatenate([v * beta[..., None], kb * jnp.exp(gam)[..., None]], axis=-1)
    sol = lax.linalg.triangular_solve(m, rhs, left_side=True, lower=True, unit_diagonal=True)
    u, w = sol[..., :dv], sol[..., dv:]
    attn = jnp.einsum('bhcid,bhcjd->bhcij', q, k) * decay

    def step(S, xs):
        qi, ki, ui, wi, gi, ai = xs
        v_new = ui - jnp.einsum('bhid,bhdv->bhiv', wi, S)
        o = jnp.einsum('bhid,bhdv->bhiv', qi * jnp.exp(gi)[..., None], S) + jnp.einsum('bhij,bhjv->bhiv', ai, v_new)
        gl = gi[..., -1:]
        S = S * jnp.exp(gl)[..., None] + jnp.einsum('bhid,bhiv->bhdv', ki * jnp.exp(gl - gi)[..., None], v_new)
        return S, o

    xs = tuple(jnp.moveaxis(t, 2, 0) for t in (q, k, u, w, gam, attn))
    state, o = lax.scan(step, state, xs)
    return jnp.moveaxis(o, 0, 2).reshape(B, H, T, dv), state


def gdn_mixer(lat, ctx, conv_w, a_log, dt_bias, norm_gain, with_ctx):
    f32 = jnp.float32

    def prep(qkv, gates):
        bsz, t_len, _ = qkv.shape
        t = jax.nn.silu(short_conv(qkv, conv_w))
        q, k, v = jnp.split(t, 3, axis=-1)
        heads = lambda a: a.reshape(bsz, t_len, GDN_HEADS, HEAD_DIM).transpose(0, 2, 1, 3).astype(f32)
        q = l2_normalize(heads(q)) * HEAD_DIM ** -0.5
        k = l2_normalize(heads(k))
        gt = gates.reshape(bsz, t_len, 4, GDN_HEADS).astype(f32).transpose(2, 0, 3, 1)
        beta = jax.nn.sigmoid(gt[0::2])
        logdecay = -jnp.exp(a_log.astype(f32))[:, None, :, None] * jax.nn.softplus(
            gt[1::2] + dt_bias.astype(f32)[:, None, :, None])
        return q, k, heads(v), beta, logdecay

    ql, kl, vl, bl, gl = prep(lat[0], lat[2])
    qc, kc, vc, bc, gc = prep(ctx[0], ctx[2])
    state0 = jnp.zeros(ql.shape[:2] + (HEAD_DIM, HEAD_DIM), f32)
    o_lat, o_ctx = [], []
    for d in range(2):
        rev = (lambda a: jnp.flip(a, axis=2)) if d == 1 else (lambda a: a)
        oc, st = gated_delta_chunkwise(rev(qc), rev(kc), rev(vc), rev(bc[d]), rev(gc[d]), state0)
        ol, _ = gated_delta_chunkwise(rev(ql), rev(kl), rev(vl), rev(bl[d]), rev(gl[d]), st)
        o_lat.append(rev(ol))
        o_ctx.append(rev(oc))

    def post(o, z):
        bsz, t_len, _ = z.shape
        o = rms_norm(o.transpose(0, 2, 1, 3), norm_gain)
        zh = z.reshape(bsz, t_len, GDN_HEADS, HEAD_DIM).astype(f32)
        return (o * jax.nn.silu(zh)).reshape(bsz, t_len, GROUP_WIDTH).astype(z.dtype)

    y_lat = post(o_lat[0] + o_lat[1], lat[1])
    return y_lat, (post(o_ctx[0] + o_ctx[1], ctx[1]) if with_ctx else None)


def swiglu(h, wg, wu, wd):
    return (jax.nn.silu(h @ wg) * (h @ wu)) @ wd


def moe_swiglu(h, router, wg, wu, wd):
    logits = (h @ router).astype(jnp.float32)
    top_val, top_idx = lax.top_k(logits, TOP_K)
    top_w = jax.nn.softmax(top_val, axis=-1)
    gates = jnp.sum(jax.nn.one_hot(top_idx, N_EXPERTS, dtype=jnp.float32) * top_w[..., None], axis=-2)
    out = jnp.zeros_like(h)
    for e in range(N_EXPERTS):
        out = out + gates[..., e:e + 1].astype(h.dtype) * swiglu(h, wg[e], wu[e], wd[e])
    return out


def setup_inputs(seed: int = 0) -> dict:
    key = jax.random.key(seed)
    ks = iter(jax.random.split(key, 40))
    D = D_MODEL
    n_dense = (DEPTH + 1) // 2
    n_moe = DEPTH // 2
    nrm = lambda shape, scale: jax.random.normal(next(ks), shape, jnp.float32) * scale
    gain = lambda shape: 1.0 + nrm(shape, 0.02)
    forget_lin = jnp.linspace(3.0, 6.0, MLSTM_HEADS, dtype=jnp.float32)
    gate_base = jnp.zeros((4, MLSTM_HEADS), jnp.float32).at[1].set(forget_lin).at[3].set(forget_lin)
    dt = jnp.exp(jax.random.uniform(next(ks), (DEPTH, 2, GDN_HEADS), jnp.float32,
                                    minval=math.log(1e-3), maxval=math.log(1e-1)))
    return {
        'x': nrm((BATCH, SEQ, D), 1.0),
        'c': nrm((BATCH, D), 1.0),
        'ctx': nrm((BATCH, CTX_LEN, D), 1.0),
        'c_ctx': nrm((D,), 1.0),
        'w_mod': nrm((DEPTH, D, 6 * D), 0.5 * D ** -0.5),
        'b_mod': nrm((DEPTH, 6 * D), 0.02),
        'norm1': gain((DEPTH, D)),
        'norm2': gain((DEPTH, D)),
        'w_in': nrm((DEPTH, D, N_IN), D ** -0.5),
        'w_out': nrm((DEPTH, MIX_WIDTH, D), MIX_WIDTH ** -0.5),
        'diff_qk_gain': gain((DEPTH, 2, DIFF_QK_DIM)),
        'diff_lambda': nrm((DEPTH, 4, DIFF_QK_DIM), 0.1),
        'diff_subln': gain((DEPTH, 2 * DIFF_QK_DIM)),
        'gqa_qk_gain': gain((DEPTH, 2, HEAD_DIM)),
        'mlstm_gate_bias': gate_base + nrm((DEPTH, 4, MLSTM_HEADS), 0.1),
        'mlstm_norm': gain((DEPTH, GROUP_WIDTH)),
        'gdn_conv': nrm((DEPTH, GDN_CONV, 3 * GROUP_WIDTH), GDN_CONV ** -0.5),
        'gdn_a_log': jnp.log(jax.random.uniform(next(ks), (DEPTH, 2, GDN_HEADS), jnp.float32, minval=1.0, maxval=16.0)),
        'gdn_dt_bias': dt + jnp.log(-jnp.expm1(-dt)),
        'gdn_norm': gain((DEPTH, HEAD_DIM)),
        'ffn_w_gate': nrm((n_dense, D, D_FF), D ** -0.5),
        'ffn_w_up': nrm((n_dense, D, D_FF), D ** -0.5),
        'ffn_w_down': nrm((n_dense, D_FF, D), D_FF ** -0.5),
        'moe_router': nrm((n_moe, D, N_EXPERTS), D ** -0.5),
        'moe_w_gate': nrm((n_moe, N_EXPERTS, D, D_FF_EXPERT), D ** -0.5),
        'moe_w_up': nrm((n_moe, N_EXPERTS, D, D_FF_EXPERT), D ** -0.5),
        'moe_w_down': nrm((n_moe, N_EXPERTS, D_FF_EXPERT, D), D_FF_EXPERT ** -0.5),
    }


def reference(x, c, ctx, c_ctx, w_mod, b_mod, norm1, norm2, w_in, w_out,
              diff_qk_gain, diff_lambda, diff_subln, gqa_qk_gain,
              mlstm_gate_bias, mlstm_norm, gdn_conv, gdn_a_log, gdn_dt_bias, gdn_norm,
              ffn_w_gate, ffn_w_up, ffn_w_down, moe_router, moe_w_gate, moe_w_up, moe_w_down):
    n = x.shape[1]
    rope_diff = axial_rope_angles(n, DIFF_QK_DIM)
    rope_gqa = axial_rope_angles(n, HEAD_DIM)
    offsets = np.cumsum(SPLIT_SIZES)[:-1].tolist()
    xc = ctx
    for li in range(DEPTH):
        with_ctx = li < DEPTH - 1
        lambda_init = 0.8 - 0.6 * math.exp(-0.3 * li)
        mod = jnp.einsum('bd,de->be', jax.nn.silu(c), w_mod[li]) + b_mod[li]
        sh1, sc1, g1, sh2, sc2, g2 = jnp.split(mod[:, None, :], 6, axis=-1)
        csh1, csc1, cg1, csh2, csc2, cg2 = jnp.split(jax.nn.silu(c_ctx) @ w_mod[li] + b_mod[li], 6, axis=-1)
        u = jnp.split(modulate(rms_norm(x, norm1[li]), sh1, sc1) @ w_in[li], offsets, axis=-1)
        uc = jnp.split(modulate(rms_norm(xc, norm1[li]), csh1, csc1) @ w_in[li], offsets, axis=-1)
        ya, ya_c = diff_attention(u[0], u[1], u[2], uc[0], uc[1], uc[2], diff_qk_gain[li], diff_lambda[li],
                                  diff_subln[li], lambda_init, rope_diff, with_ctx)
        yb, yb_c = gqa_attention(u[3], u[4], u[5], uc[3], uc[4], uc[5], gqa_qk_gain[li], rope_gqa, with_ctx)
        yc, yc_c = mlstm_mixer(u[6:11], uc[6:11], mlstm_gate_bias[li], mlstm_norm[li], with_ctx)
        yd, yd_c = gdn_mixer(u[11:14], uc[11:14], gdn_conv[li], gdn_a_log[li], gdn_dt_bias[li], gdn_norm[li], with_ctx)
        x = x + g1 * (jnp.concatenate([ya, yb, yc, yd], axis=-1) @ w_out[li])
        if with_ctx:
            xc = xc + cg1 * (jnp.concatenate([ya_c, yb_c, yc_c, yd_c], axis=-1) @ w_out[li])
        j = li // 2
        if li % 2 == 0:
            ffn = lambda h: swiglu(h, ffn_w_gate[j], ffn_w_up[j], ffn_w_down[j])
        else:
            ffn = lambda h: moe_swiglu(h, moe_router[j], moe_w_gate[j], moe_w_up[j], moe_w_down[j])
        x = x + g2 * ffn(modulate(rms_norm(x, norm2[li]), sh2, sc2))
        if with_ctx:
            xc = xc + cg2 * ffn(modulate(rms_norm(xc, norm2[li]), csh2, csc2))
    return x
```

```python
import functools
import math

import numpy as np
import jax
import jax.numpy as jnp
from jax import lax
from jax.experimental import pallas as pl
from jax.experimental.pallas import tpu as pltpu

F32 = jnp.float32
BF16 = jnp.bfloat16
EPS = 1e-6
ROPE_THETA = 10000.0
GRID_W = 64
LOG2E = 1.4426950408889634

TM = 256
CH = 64
HD = 64
GW = 256
LANES = 128
VMEM_LIMIT = 56 * 1024 * 1024

NN = (((1,), (0,)), ((), ()))
NT = (((1,), (1,)), ((), ()))
TN = (((0,), (0,)), ((), ()))


def _dot(a, b, dims=NN):
    return lax.dot_general(a.astype(BF16), b.astype(BF16), dims, preferred_element_type=F32)


def _split(a):
    hi = a.astype(BF16)
    lo = (a - hi.astype(F32)).astype(BF16)
    return hi, lo


def _dot_xl(a, b_exact, dims=NN):
    hi, lo = _split(a)
    bb = b_exact.astype(BF16)
    return (lax.dot_general(hi, bb, dims, preferred_element_type=F32)
            + lax.dot_general(lo, bb, dims, preferred_element_type=F32))


def _dot_xr(a_exact, b, dims=NN):
    hi, lo = _split(b)
    aa = a_exact.astype(BF16)
    return (lax.dot_general(aa, hi, dims, preferred_element_type=F32)
            + lax.dot_general(aa, lo, dims, preferred_element_type=F32))


def _dot_x3(a, b, dims=NN):
    ah, al = _split(a)
    bh, bl = _split(b)
    d = functools.partial(lax.dot_general, dimension_numbers=dims, preferred_element_type=F32)
    return d(ah, bh) + d(ah, bl) + d(al, bh)


def _sigmoid(x):
    return 1.0 / (1.0 + jnp.exp(-x))


def _silu(x):
    return x * _sigmoid(x)


def _softplus(x):
    return jnp.maximum(x, 0.0) + jnp.log(1.0 + jnp.exp(-jnp.abs(x)))


def _cparams(sem, vmem=VMEM_LIMIT):
    return pltpu.CompilerParams(dimension_semantics=sem, vmem_limit_bytes=vmem)


def _mod_kernel(c_ref, w_ref, b_ref, o_ref):
    cv = c_ref[...]
    s = _silu(cv)
    o_ref[0] = lax.dot_general(s, w_ref[0], NN, precision=lax.Precision.HIGHEST,
                               preferred_element_type=F32) + b_ref[0]


def _mod_call(cvec, w_mod, b_mod):
    depth, d, n6 = w_mod.shape
    rows = cvec.shape[0]
    tn = 1536
    return pl.pallas_call(
        _mod_kernel,
        grid=(depth, n6 // tn),
        in_specs=[pl.BlockSpec((rows, d), lambda l, j: (0, 0)),
                  pl.BlockSpec((1, d, tn), lambda l, j: (l, 0, j)),
                  pl.BlockSpec((1, 1, tn), lambda l, j: (l, 0, j))],
        out_specs=pl.BlockSpec((1, rows, tn), lambda l, j: (l, 0, j)),
        out_shape=jax.ShapeDtypeStruct((depth, rows, n6), F32),
        compiler_params=_cparams(("arbitrary", "arbitrary")),
        name="adaln_mod",
    )(cvec, w_mod, b_mod.reshape(depth, 1, n6))


def _norm_mod(x, gain, shift, scale):
    ms = jnp.mean(x * x, axis=-1, keepdims=True)
    return (x * lax.rsqrt(ms + EPS) * gain) * (1.0 + scale) + shift


def _group_mean(v, g_ref):
    return _dot_xl(v, g_ref[...])


def _rope(xn, cos, sin, half):
    w = xn.shape[1]
    lane = lax.broadcasted_iota(jnp.int32, xn.shape, 1)
    first = (lane % (2 * half)) < half
    xs = jnp.where(first, pltpu.roll(xn, w - half, 1), pltpu.roll(xn, half, 1))
    return xn * cos + xs * sin


A_W = 3 * GW
B_W = GW + 2 * 128
AB_W = A_W + B_W
U_W = 2176
U_DQKV, U_CQ, U_CK, U_CV, U_CO, U_DZ, U_G = 0, 768, 1024, 1280, 1536, 1792, 2048


def _kin_kernel(x_ref, mod_ref, gain_ref, w_ref, g32_ref, g64_ref, qkg_ref,
                cosa_ref, sina_ref, cosb_ref, sinb_ref,
                qta_ref, ka_ref, vta_ref, qtb_ref, kb_ref, vtb_ref, u_ref):
    x = x_ref[...]
    mod = mod_ref[0, 0]
    h = _norm_mod(x, gain_ref[...], mod[0:1], mod[1:2]).astype(BF16)

    ya = lax.dot_general(h, w_ref[:, 0:A_W], NN, preferred_element_type=F32)
    qkg = qkg_ref[...]
    ca, sa = cosa_ref[...], sina_ref[...]

    def prep(v, g_ref, gain, cos, sin, half):
        vn = v * lax.rsqrt(_group_mean(v * v, g_ref) + EPS) * gain
        return _rope(vn, cos, sin, half)

    qa = prep(ya[:, 0:GW], g32_ref, qkg[0:1], ca, sa, 8) * ((HD // 2) ** -0.5 * LOG2E)
    ka = prep(ya[:, GW:2 * GW], g32_ref, qkg[1:2], ca, sa, 8)
    qta_ref[0, 0] = qa.T.astype(BF16)
    ka_ref[...] = ka.astype(BF16)
    vta_ref[0, 0] = ya[:, 2 * GW:3 * GW].T.astype(BF16)

    yb = lax.dot_general(h, w_ref[:, A_W:AB_W], NN, preferred_element_type=F32)
    cb, sb = cosb_ref[...], sinb_ref[...]
    qb = prep(yb[:, 0:GW], g64_ref, qkg[2:3], cb, sb, 16) * (HD ** -0.5 * LOG2E)
    kb = prep(yb[:, GW:GW + 128], g64_ref.at[0:128, 0:128], qkg[3:4, 0:128], cb[:, 0:128], sb[:, 0:128], 16)
    qtb_ref[0, 0] = qb.T.astype(BF16)
    kb_ref[...] = kb.astype(BF16)
    vtb_ref[0, 0] = yb[:, GW + 128:GW + 256].T.astype(BF16)

    u_ref[...] = lax.dot_general(h, w_ref[:, AB_W:AB_W + U_W], NN, preferred_element_type=F32)


def _kin_call(X, modtab, gain, w_r, g32, g64, qkg, cosa, sina, cosb, sinb, B, T):
    nt = T // TM
    d = X.shape[1]
    wcols = w_r.shape[1]
    row = lambda b, i: (b * nt + i, 0)
    const = lambda b, i: (0, 0)
    pos = lambda b, i: (i, 0)
    tmap = lambda b, i: (b, i, 0, 0)
    in_specs = [
        pl.BlockSpec((TM, d), row),
        pl.BlockSpec((1, 1, 8, d), lambda b, i: (b, jnp.minimum(i, 1), 0, 0)),
        pl.BlockSpec((1, d), const),
        pl.BlockSpec((d, wcols), const),
        pl.BlockSpec((GW, GW), const),
        pl.BlockSpec((GW, GW), const),
        pl.BlockSpec((8, GW), const),
        pl.BlockSpec((TM, GW), pos), pl.BlockSpec((TM, GW), pos),
        pl.BlockSpec((TM, GW), pos), pl.BlockSpec((TM, GW), pos),
    ]
    out_specs = [
        pl.BlockSpec((1, 1, GW, TM), tmap), pl.BlockSpec((TM, GW), row), pl.BlockSpec((1, 1, GW, TM), tmap),
        pl.BlockSpec((1, 1, GW, TM), tmap), pl.BlockSpec((TM, 128), row), pl.BlockSpec((1, 1, 128, TM), tmap),
        pl.BlockSpec((TM, U_W), row),
    ]
    out_shape = [
        jax.ShapeDtypeStruct((B, nt, GW, TM), BF16), jax.ShapeDtypeStruct((B * T, GW), BF16),
        jax.ShapeDtypeStruct((B, nt, GW, TM), BF16),
        jax.ShapeDtypeStruct((B, nt, GW, TM), BF16), jax.ShapeDtypeStruct((B * T, 128), BF16),
        jax.ShapeDtypeStruct((B, nt, 128, TM), BF16),
        jax.ShapeDtypeStruct((B * T, U_W), F32),
    ]
    return pl.pallas_call(
        _kin_kernel, grid=(B, nt), in_specs=in_specs, out_specs=out_specs, out_shape=out_shape,
        compiler_params=_cparams(("arbitrary", "arbitrary")), name="in_proj",
    )(X, modtab, gain, w_r, g32, g64, qkg, cosa, sina, cosb, sinb)


def _attn_kernel(qt_ref, k_ref, vt_ref, lam_ref, o_ref, *, diff, lam_init, n_ctx_tiles, n_ctx_blk, n_all_blk):
    i = pl.program_id(1)
    nkv = jnp.where(i < n_ctx_tiles, n_ctx_blk, n_all_blk)
    qt = qt_ref[0, 0]
    kd = k_ref.shape[1]
    rowi = lax.broadcasted_iota(jnp.int32, (kd, TM), 0)

    def run_stream(qm, vlo):
        def body(j, carry):
            m, l, acc = carry
            off = pl.multiple_of(j * TM, TM)
            kb = k_ref[pl.ds(off, TM), :]
            s = lax.dot_general(kb, qm, NN, preferred_element_type=F32)
            m_new = jnp.maximum(m, jnp.max(s, axis=0, keepdims=True))
            alpha = jnp.exp2(m - m_new)
            p = jnp.exp2(s - m_new)
            l = alpha * l + jnp.sum(p, axis=0, keepdims=True)
            vb = vt_ref[0, j, vlo:vlo + HD, :]
            acc = alpha * acc + lax.dot_general(vb, p.astype(BF16), NN, preferred_element_type=F32)
            return m_new, l, acc

        m0 = jnp.full((1, TM), -jnp.inf, F32)
        l0 = jnp.zeros((1, TM), F32)
        a0 = jnp.zeros((HD, TM), F32)
        m, l, acc = lax.fori_loop(0, nkv, body, (m0, l0, a0))
        return acc * (1.0 / l)

    if diff:
        lv = lam_ref[...]
        lam = (jnp.exp(jnp.sum(lv[0:1] * lv[1:2], axis=1, keepdims=True))
               - jnp.exp(jnp.sum(lv[2:3] * lv[3:4], axis=1, keepdims=True)) + lam_init)
    outs = []
    for h in range(4):
        if diff:
            res = []
            for j in range(2):
                lo = (2 * h + j) * (HD // 2)
                qm = jnp.where((rowi >= lo) & (rowi < lo + HD // 2), qt, jnp.zeros_like(qt))
                res.append(run_stream(qm, h * HD))
            outs.append(res[0] - lam * res[1])
        else:
            kvh = h // 2
            qh = qt[h * HD:(h + 1) * HD, :]
            z = jnp.zeros_like(qh)
            qm = jnp.concatenate([qh, z] if kvh == 0 else [z, qh], axis=0)
            outs.append(run_stream(qm, kvh * HD))
    o_ref[...] = jnp.concatenate(outs, axis=0).T


def _attn_call(qt, k, vt, lam, B, T, CT, diff, lam_init, name):
    nt = T // TM
    kd = k.shape[1]
    vr = vt.shape[2]
    kern = functools.partial(_attn_kernel, diff=diff, lam_init=lam_init, n_ctx_tiles=CT // TM,
                             n_ctx_blk=CT // TM, n_all_blk=nt)
    return pl.pallas_call(
        kern, grid=(B, nt),
        in_specs=[pl.BlockSpec((1, 1, GW, TM), lambda b, i: (b, i, 0, 0)),
                  pl.BlockSpec((T, kd), lambda b, i: (b, 0)),
                  pl.BlockSpec((1, nt, vr, TM), lambda b, i: (b, 0, 0, 0)),
                  pl.BlockSpec(lam.shape, lambda b, i: (0, 0))],
        out_specs=pl.BlockSpec((TM, GW), lambda b, i: (b * nt + i, 0)),
        out_shape=jax.ShapeDtypeStruct((B * T, GW), F32),
        compiler_params=_cparams(("arbitrary", "arbitrary")), name=name,
    )(qt, k, vt, lam)


def _chunk_masks(reverse):
    row = lax.broadcasted_iota(jnp.int32, (CH, GW), 0)
    lane = lax.broadcasted_iota(jnp.int32, (CH, GW), 1)
    sidx = lane % CH
    eye = row == sidx
    causal = (sidx >= row) if reverse else (sidx <= row)
    strict = (sidx > row) if reverse else (sidx < row)
    r2 = lax.broadcasted_iota(jnp.int32, (GW, GW), 0)
    c2 = lax.broadcasted_iota(jnp.int32, (GW, GW), 1)
    blk = (r2 // CH) == (c2 // CH)
    t1 = lax.broadcasted_iota(jnp.int32, (CH, CH), 0)
    t2 = lax.broadcasted_iota(jnp.int32, (CH, CH), 1)
    tri = ((t2 >= t1) if reverse else (t2 <= t1)).astype(F32)
    return eye, causal, strict, blk, tri


def _bd(y, blk):
    return jnp.where(blk, jnp.concatenate([y, y, y, y], axis=0), 0.0)


def _row_form(col, eye):
    return _dot_xr(jnp.ones((CH, CH), F32), jnp.where(eye, col, 0.0))


def _cummax(a, reverse):
    n = a.shape[0]
    row = lax.broadcasted_iota(jnp.int32, a.shape, 0)
    k = 1
    while k < n:
        if reverse:
            sh, valid = pltpu.roll(a, n - k, 0), row < n - k
        else:
            sh, valid = pltpu.roll(a, k, 0), row >= k
        a = jnp.maximum(a, jnp.where(valid, sh, -jnp.inf))
        k *= 2
    return a


def _chunk_order(p, nc_ctx, nch, reverse):
    if not reverse:
        return p
    return jnp.where(p < nc_ctx, nc_ctx - 1 - p, nch + nc_ctx - 1 - p)


def _mlstm_kernel(q_ref, k_ref, v_ref, g_ref, bias_ref, ei_ref, ef_ref, h_ref, c_sc, n_sc, m_sc, *, reverse):
    @pl.when(pl.program_id(1) == 0)
    def _():
        c_sc[...] = jnp.zeros_like(c_sc)
        n_sc[...] = jnp.zeros_like(n_sc)
        m_sc[...] = jnp.zeros_like(m_sc)

    eye, causal, _, blk, tri = _chunk_masks(reverse)
    last = 0 if reverse else CH - 1
    q4 = q_ref[...]
    k4 = k_ref[...] * (HD ** -0.5)
    v4 = v_ref[...]
    gt = g_ref[...] + bias_ref[...]
    i_e = _dot_xl(gt, ei_ref[0])
    f_e = _dot_xl(gt, ef_ref[0])
    logf = jnp.minimum(f_e, 0.0) - jnp.log(1.0 + jnp.exp(-jnp.abs(f_e)))
    b_e = _dot_xr(tri, logf)
    m_prev = m_sc[0:1, :]
    a_e = i_e - b_e
    inter = b_e + m_prev
    m_t = jnp.maximum(inter, b_e + _cummax(a_e, reverse))
    w_inter = jnp.exp(inter - m_t)
    dexp = jnp.exp(jnp.where(causal, b_e - m_t + _row_form(a_e, eye), -jnp.inf))
    kbd = _bd(k4, blk)
    smat = _dot(q4, kbd, NT) * dexp
    cbd = c_sc[...]
    nbd = n_sc[...]
    num = w_inter * _dot(q4, cbd) + _dot(smat, _bd(v4, blk))
    den = w_inter * _dot(q4, nbd) + _dot_xl(smat, blk.astype(F32))
    h_ref[...] = num / jnp.maximum(jnp.abs(den), jnp.exp(-m_t))

    b_last = b_e[last:last + 1, :]
    g = b_last - b_e + i_e
    m_new = jnp.maximum(b_last + m_prev, jnp.max(g, axis=0, keepdims=True))
    a_prev = jnp.exp(b_last + m_prev - m_new)
    ks = k4 * jnp.exp(g - m_new)
    c_sc[...] = a_prev * cbd + jnp.where(blk, _dot(ks, v4, TN), 0.0)
    n_sc[...] = a_prev * nbd + jnp.where(blk, _dot(ks, jnp.ones((CH, GW), F32), TN), 0.0)
    m_sc[...] = jnp.broadcast_to(m_new, m_sc.shape)


def _mlstm_call(U, bias128, e_i, e_f, B, T, CT, reverse):
    nch, ncc = T // CH, CT // CH
    cmap = lambda col: (lambda b, p: (b * nch + _chunk_order(p, ncc, nch, reverse), col))
    d = 1 if reverse else 0
    return pl.pallas_call(
        functools.partial(_mlstm_kernel, reverse=reverse), grid=(B, nch),
        in_specs=[pl.BlockSpec((CH, GW), cmap(U_CQ // GW)),
                  pl.BlockSpec((CH, GW), cmap(U_CK // GW)),
                  pl.BlockSpec((CH, GW), cmap(U_CV // GW)),
                  pl.BlockSpec((CH, LANES), cmap(U_G // LANES)),
                  pl.BlockSpec((1, LANES), lambda b, p: (0, 0)),
                  pl.BlockSpec((1, LANES, GW), lambda b, p: (d, 0, 0)),
                  pl.BlockSpec((1, LANES, GW), lambda b, p: (d, 0, 0))],
        out_specs=pl.BlockSpec((CH, GW), cmap(0)),
        out_shape=jax.ShapeDtypeStruct((B * T, GW), F32),
        scratch_shapes=[pltpu.VMEM((GW, GW), F32), pltpu.VMEM((GW, GW), F32), pltpu.VMEM((8, GW), F32)],
        compiler_params=_cparams(("arbitrary", "arbitrary")),
        name="mlstm_bwd" if reverse else "mlstm_fwd",
    )(U, U, U, U, bias128, e_i, e_f)


def _gdn_prep_kernel(prev_ref, cur_ref, next_ref, g_ref, conv_ref, g64_ref, e_ref, alog_ref, dtb_ref,
                     q_ref, k_ref, v_ref, bf_ref, gf_ref, bb_ref, gb_ref, *, n_ctx_tiles, nt):
    i = pl.program_id(1)
    at_start = (i == 0) | (i == n_ctx_tiles)
    at_end = (i == n_ctx_tiles - 1) | (i == nt - 1)
    cur = cur_ref[...]
    head = jnp.where(at_start, 0.0, prev_ref[TM - 8:TM, :])
    tail = jnp.where(at_end, 0.0, next_ref[0:8, :])
    xx = jnp.concatenate([head, cur, tail], axis=0)
    cw = conv_ref[...]
    kw = cw.shape[0]
    acc = jnp.zeros_like(cur)
    for j in range(kw):
        s0 = 8 + j - kw // 2
        acc = acc + xx[s0:s0 + TM, :] * cw[j:j + 1, :]
    t = _silu(acc)
    q, k, v = t[:, 0:GW], t[:, GW:2 * GW], t[:, 2 * GW:3 * GW]
    g64 = g64_ref[...] * float(HD)
    q_ref[...] = q * lax.rsqrt(_dot_xl(q * q, g64) + EPS) * (HD ** -0.5)
    k_ref[...] = k * lax.rsqrt(_dot_xl(k * k, g64) + EPS)
    v_ref[...] = v
    gt = g_ref[...]
    for d, (b_out, g_out) in enumerate(((bf_ref, gf_ref), (bb_ref, gb_ref))):
        b_out[...] = _sigmoid(_dot_xl(gt, e_ref[2 * d]))
        a_e = _dot_xl(gt, e_ref[2 * d + 1])
        g_out[...] = -jnp.exp(alog_ref[d:d + 1, :]) * _softplus(a_e + dtb_ref[d:d + 1, :])


def _gdn_prep_call(U, conv_w, g64, e_d, alog_e, dtb_e, B, T, CT):
    nt = T // TM
    w3 = 3 * GW
    row = lambda b, i: (b * nt + i, 0)
    const2 = lambda b, i: (0, 0)
    o = jax.ShapeDtypeStruct((B * T, GW), F32)
    return pl.pallas_call(
        functools.partial(_gdn_prep_kernel, n_ctx_tiles=CT // TM, nt=nt), grid=(B, nt),
        in_specs=[pl.BlockSpec((TM, w3), lambda b, i: (b * nt + jnp.maximum(i - 1, 0), 0)),
                  pl.BlockSpec((TM, w3), row),
                  pl.BlockSpec((TM, w3), lambda b, i: (b * nt + jnp.minimum(i + 1, nt - 1), 0)),
                  pl.BlockSpec((TM, LANES), lambda b, i: (b * nt + i, U_G // LANES)),
                  pl.BlockSpec(conv_w.shape, const2),
                  pl.BlockSpec((GW, GW), const2),
                  pl.BlockSpec(e_d.shape, lambda b, i: (0, 0, 0)),
                  pl.BlockSpec((2, GW), const2), pl.BlockSpec((2, GW), const2)],
        out_specs=[pl.BlockSpec((TM, GW), row)] * 7,
        out_shape=[o] * 7,
        compiler_params=_cparams(("arbitrary", "arbitrary")), name="gdn_prep",
    )(U, U, U, U, conv_w, g64, e_d, alog_e, dtb_e)


def _gdn_kernel(q_ref, k_ref, v_ref, beta_ref, g_ref, o_ref, s_sc, *, reverse):
    @pl.when(pl.program_id(1) == 0)
    def _():
        s_sc[...] = jnp.zeros_like(s_sc)

    eye, causal, strict, blk, tri = _chunk_masks(reverse)
    last = 0 if reverse else CH - 1
    q4, k4, v4, beta = q_ref[...], k_ref[...], v_ref[...], beta_ref[...]
    gam = _dot_xr(tri, g_ref[...])
    decay = jnp.exp(jnp.where(causal, gam - _row_form(gam, eye), -jnp.inf))
    kbeta = k4 * beta
    kbd = _bd(k4, blk)
    a4 = jnp.where(strict, _dot(kbeta, kbd, NT) * decay, 0.0)
    attn = _dot(q4, kbd, NT) * decay
    tinv = jnp.where(eye, 1.0, 0.0) - a4
    apow = a4
    for _ in range(5):
        apow = _dot_x3(apow, _bd(apow, blk))
        tinv = tinv + _dot_x3(tinv, _bd(apow, blk))
    u4 = _dot_x3(tinv, _bd(v4 * beta, blk))
    w4 = _dot_x3(tinv, _bd(kbeta * jnp.exp(gam), blk))
    sbd = s_sc[...]
    v_new = u4 - _dot(w4, sbd)
    o_ref[...] = _dot(q4 * jnp.exp(gam), sbd) + _dot(attn, _bd(v_new, blk))
    gl = gam[last:last + 1, :]
    s_sc[...] = sbd * jnp.exp(gl) + jnp.where(blk, _dot(k4 * jnp.exp(gl - gam), v_new, TN), 0.0)


def _gdn_call(q, k, v, beta, g, B, T, CT, reverse):
    nch, ncc = T // CH, CT // CH
    cmap = lambda b, p: (b * nch + _chunk_order(p, ncc, nch, reverse), 0)
    spec = pl.BlockSpec((CH, GW), cmap)
    return pl.pallas_call(
        functools.partial(_gdn_kernel, reverse=reverse), grid=(B, nch),
        in_specs=[spec] * 5, out_specs=spec,
        out_shape=jax.ShapeDtypeStruct((B * T, GW), F32),
        scratch_shapes=[pltpu.VMEM((GW, GW), F32)],
        compiler_params=_cparams(("arbitrary", "arbitrary")),
        name="gdn_bwd" if reverse else "gdn_fwd",
    )(q, k, v, beta, g)


def _post_kernel(x_ref, mod_ref, oa_ref, ob_ref, hcf_ref, hcb_ref, odf_ref, odb_ref, oc_ref, zd_ref,
                 gains_ref, g64_ref, w_ref, xo_ref, *, lam_init):
    gains = gains_ref[...]
    g64 = g64_ref

    oa = oa_ref[...]
    ya = oa * lax.rsqrt(_group_mean(oa * oa, g64) + EPS) * gains[0:1] * (1.0 - lam_init)
    yb = ob_ref[...]
    hc = hcf_ref[...] + hcb_ref[...]
    dc = hc - _group_mean(hc, g64)
    yc = _sigmoid(oc_ref[...]) * (dc * lax.rsqrt(_group_mean(dc * dc, g64) + EPS) * gains[1:2])
    od = odf_ref[...] + odb_ref[...]
    yd = od * lax.rsqrt(_group_mean(od * od, g64) + EPS) * gains[2:3] * _silu(zd_ref[...])
    out = _dot(ya, w_ref[0:GW, :])
    out = out + _dot(yb, w_ref[GW:2 * GW, :])
    out = out + _dot(yc, w_ref[2 * GW:3 * GW, :])
    out = out + _dot(yd, w_ref[3 * GW:4 * GW, :])
    xo_ref[...] = x_ref[...] + mod_ref[0, 0][2:3] * out


def _post_call(X, modtab, oa, ob, hcf, hcb, odf, odb, U, gains, g64, w_out, B, T, lam_init):
    nt = T // TM
    d = X.shape[1]
    row = lambda b, i: (b * nt + i, 0)
    const = lambda b, i: (0, 0)
    gspec = pl.BlockSpec((TM, GW), row)
    return pl.pallas_call(
        functools.partial(_post_kernel, lam_init=lam_init), grid=(B, nt),
        in_specs=[pl.BlockSpec((TM, d), row),
                  pl.BlockSpec((1, 1, 8, d), lambda b, i: (b, jnp.minimum(i, 1), 0, 0)),
                  gspec, gspec, gspec, gspec, gspec, gspec,
                  pl.BlockSpec((TM, GW), lambda b, i: (b * nt + i, U_CO // GW)),
                  pl.BlockSpec((TM, GW), lambda b, i: (b * nt + i, U_DZ // GW)),
                  pl.BlockSpec((8, GW), const), pl.BlockSpec((GW, GW), const),
                  pl.BlockSpec(w_out.shape, const)],
        out_specs=pl.BlockSpec((TM, d), row),
        out_shape=jax.ShapeDtypeStruct(X.shape, F32),
        compiler_params=_cparams(("arbitrary", "arbitrary")), name="out_proj",
    )(X, modtab, oa, ob, hcf, hcb, odf, odb, U, U, gains, g64, w_out)


FF_CHUNK = 256


def _ffn_kernel(x_ref, mod_ref, gain_ref, wg_ref, wu_ref, wd_ref, xo_ref):
    x = x_ref[...]
    mod = mod_ref[0, 0]
    h = _norm_mod(x, gain_ref[...], mod[3:4], mod[4:5]).astype(BF16)
    acc = jnp.zeros(x.shape, F32)
    for f in range(0, wg_ref.shape[1], FF_CHUNK):
        g = lax.dot_general(h, wg_ref[:, f:f + FF_CHUNK], NN, preferred_element_type=F32)
        u = lax.dot_general(h, wu_ref[:, f:f + FF_CHUNK], NN, preferred_element_type=F32)
        acc = acc + _dot(_silu(g) * u, wd_ref[f:f + FF_CHUNK, :])
    xo_ref[...] = x + mod[5:6] * acc


def _ffn_call(X, modtab, gain, wg, wu, wd, nb, tiles_per_b, kind_of_tile):
    d = X.shape[1]
    row = lambda b, i: (b * tiles_per_b + i, 0)
    const = lambda b, i: (0, 0)
    return pl.pallas_call(
        _ffn_kernel, grid=(nb, tiles_per_b),
        in_specs=[pl.BlockSpec((TM, d), row),
                  pl.BlockSpec((1, 1, 8, d), lambda b, i: (b, kind_of_tile(i), 0, 0)),
                  pl.BlockSpec((1, d), const),
                  pl.BlockSpec(wg.shape, const), pl.BlockSpec(wu.shape, const), pl.BlockSpec(wd.shape, const)],
        out_specs=pl.BlockSpec((TM, d), row),
        out_shape=jax.ShapeDtypeStruct(X.shape, F32),
        compiler_params=_cparams(("arbitrary", "arbitrary")), name="ffn_dense",
    )(X, modtab, gain, wg, wu, wd)


def _router_kernel(x_ref, mod_ref, gain_ref, r_ref, h_ref, gates_ref, *, n_exp):
    x = x_ref[...]
    mod = mod_ref[0, 0]
    h = _norm_mod(x, gain_ref[...], mod[3:4], mod[4:5])
    h_ref[...] = h.astype(BF16)
    logits = lax.dot_general(h, r_ref[...], NN, precision=lax.Precision.HIGHEST, preferred_element_type=F32)
    lane = lax.broadcasted_iota(jnp.int32, logits.shape, 1)
    neg = -jnp.inf
    lg = jnp.where(lane < n_exp, logits, neg)
    v1 = jnp.max(lg, axis=1, keepdims=True)
    i1 = jnp.min(jnp.where(lg == v1, lane, LANES), axis=1, keepdims=True)
    lg2 = jnp.where(lane == i1, neg, lg)
    v2 = jnp.max(lg2, axis=1, keepdims=True)
    i2 = jnp.min(jnp.where(lg2 == v2, lane, LANES), axis=1, keepdims=True)
    e2 = jnp.exp(v2 - v1)
    w1 = 1.0 / (1.0 + e2)
    w2 = e2 / (1.0 + e2)
    gates_ref[...] = jnp.where(lane == i1, w1, 0.0) + jnp.where(lane == i2, w2, 0.0)


def _router_call(X, modtab, gain, router128, nb, tiles_per_b, n_exp):
    d = X.shape[1]
    row = lambda b, i: (b * tiles_per_b + i, 0)
    const = lambda b, i: (0, 0)
    return pl.pallas_call(
        functools.partial(_router_kernel, n_exp=n_exp), grid=(nb, tiles_per_b),
        in_specs=[pl.BlockSpec((TM, d), row),
                  pl.BlockSpec((1, 1, 8, d), lambda b, i: (b, 1, 0, 0)),
                  pl.BlockSpec((1, d), const),
                  pl.BlockSpec(router128.shape, const)],
        out_specs=[pl.BlockSpec((TM, d), row), pl.BlockSpec((TM, LANES), row)],
        out_shape=[jax.ShapeDtypeStruct(X.shape, BF16), jax.ShapeDtypeStruct((X.shape[0], LANES), F32)],
        compiler_params=_cparams(("arbitrary", "arbitrary")), name="moe_router",
    )(X, modtab, gain, router128)


MOE_TM = 1024
MOE_TF = 512


def _moe_kernel(x_ref, mod_ref, h_ref, gates_ref, wg_ref, wu_ref, wd_ref, xo_ref, acc_sc):
    e = pl.program_id(2)
    f = pl.program_id(3)
    ne = pl.num_programs(2)
    nf = pl.num_programs(3)

    @pl.when((e == 0) & (f == 0))
    def _():
        acc_sc[...] = jnp.zeros_like(acc_sc)

    h = h_ref[...]
    gates = gates_ref[...]
    lane = lax.broadcasted_iota(jnp.int32, gates.shape, 1)
    gate = jnp.sum(jnp.where(lane == e, gates, 0.0), axis=1, keepdims=True)
    g = lax.dot_general(h, wg_ref[0], NN, preferred_element_type=F32)
    u = lax.dot_general(h, wu_ref[0], NN, preferred_element_type=F32)
    acc_sc[...] += _dot(_silu(g) * u * gate, wd_ref[0])

    @pl.when((e == ne - 1) & (f == nf - 1))
    def _():
        xo_ref[...] = x_ref[...] + mod_ref[0, 0][5:6] * acc_sc[...]


def _moe_call(X, modtab, hb, gates, wg, wu, wd, nb, rows_per_b):
    d = X.shape[1]
    n_exp, _, ff = wg.shape
    tm = min(MOE_TM, rows_per_b)
    tpb = rows_per_b // tm
    row = lambda b, i, e, f: (b * tpb + i, 0)
    return pl.pallas_call(
        _moe_kernel, grid=(nb, tpb, n_exp, ff // MOE_TF),
        in_specs=[pl.BlockSpec((tm, d), row),
                  pl.BlockSpec((1, 1, 8, d), lambda b, i, e, f: (b, 1, 0, 0)),
                  pl.BlockSpec((tm, d), row),
                  pl.BlockSpec((tm, LANES), row),
                  pl.BlockSpec((1, d, MOE_TF), lambda b, i, e, f: (e, 0, f)),
                  pl.BlockSpec((1, d, MOE_TF), lambda b, i, e, f: (e, 0, f)),
                  pl.BlockSpec((1, MOE_TF, d), lambda b, i, e, f: (e, f, 0))],
        out_specs=pl.BlockSpec((tm, d), row),
        out_shape=jax.ShapeDtypeStruct(X.shape, F32),
        scratch_shapes=[pltpu.VMEM((tm, d), F32)],
        compiler_params=_cparams(("arbitrary",) * 4), name="moe_experts",
    )(X, modtab, hb, gates, wg, wu, wd)


def _group_mean_matrix(group):
    idx = np.arange(GW)
    return jnp.asarray((idx[:, None] // group == idx[None, :] // group).astype(np.float32) / group, BF16)


def _rope_tables(n_lat, n_ctx, dim, reps):
    t = jnp.arange(n_lat)
    pos = jnp.stack([(t // GRID_W).astype(F32), (t % GRID_W).astype(F32)], axis=1)
    n_freq = dim // 4
    inv_freq = ROPE_THETA ** (-jnp.arange(n_freq, dtype=F32) / n_freq)
    ang = pos[:, :, None] * inv_freq
    cos = jnp.cos(ang)
    sin = jnp.sin(ang)
    cos = jnp.stack([cos, cos], axis=2).reshape(n_lat, dim)
    sin = jnp.stack([-sin, sin], axis=2).reshape(n_lat, dim)
    cos = jnp.concatenate([jnp.ones((n_ctx, dim), F32), cos], axis=0)
    sin = jnp.concatenate([jnp.zeros((n_ctx, dim), F32), sin], axis=0)
    return jnp.tile(cos, (1, reps)), jnp.tile(sin, (1, reps))


def _expand_matrix(col0):
    m = np.zeros((LANES, GW), np.float32)
    for h in range(4):
        m[col0 + h, h * HD:(h + 1) * HD] = 1.0
    return m


_SPLITS = (256, 256, 256, 256, 128, 128, 256, 256, 256, 256, 16, 768, 256, 16)


def _reorder_w_in(w):
    offs = np.concatenate([[0], np.cumsum(_SPLITS)])
    seg = [w[:, offs[i]:offs[i + 1]] for i in range(len(_SPLITS))]
    aq, ak, av, bq, bk, bv, cq, ck, cv, co, cg, dqkv, dz, dg = seg
    pad = jnp.zeros((w.shape[0], LANES - 32), w.dtype)
    return jnp.concatenate([aq, ak, av, bq, bk, bv, dqkv, cq, ck, cv, co, dz, cg, dg, pad], axis=1).astype(BF16)


def kernel(x, c, ctx, c_ctx, w_mod, b_mod, norm1, norm2, w_in, w_out, diff_qk_gain, diff_lambda, diff_subln,
           gqa_qk_gain, mlstm_gate_bias, mlstm_norm, gdn_conv, gdn_a_log, gdn_dt_bias, gdn_norm,
           ffn_w_gate, ffn_w_up, ffn_w_down, moe_router, moe_w_gate, moe_w_up, moe_w_down):
    B, S, D = x.shape
    CT = ctx.shape[1]
    T = CT + S
    depth = w_in.shape[0]
    assert CT % TM == 0 and S % TM == 0 and S % GRID_W == 0 and B + 1 <= 8

    g32 = _group_mean_matrix(HD // 2)
    g64 = _group_mean_matrix(HD)
    cosa, sina = _rope_tables(S, CT, HD // 2, 8)
    cosb, sinb = _rope_tables(S, CT, HD, 4)
    e_i = jnp.asarray(np.stack([_expand_matrix(0), _expand_matrix(8)]), F32)
    e_f = jnp.asarray(np.stack([_expand_matrix(4), _expand_matrix(12)]), F32)
    e_d = jnp.asarray(np.stack([_expand_matrix(16 + 4 * j) for j in range(4)]), F32)

    cvec = jnp.zeros((8, D), F32).at[:B].set(c).at[B].set(c_ctx)
    mod_all = _mod_call(cvec, w_mod, b_mod).reshape(depth, 8, 6, D)

    X = jnp.concatenate([ctx, x], axis=1).reshape(B * T, D)
    nb_rows = T
    for li in range(depth):
        with_ctx = li < depth - 1
        lam_init = 0.8 - 0.6 * math.exp(-0.3 * li)
        m = mod_all[li]
        modtab = jnp.stack([jnp.broadcast_to(m[B], (B, 6, D)), m[:B]], axis=1)
        modtab = jnp.pad(modtab, ((0, 0), (0, 0), (0, 2), (0, 0)))

        qkg = jnp.zeros((8, GW), F32)
        qkg = qkg.at[0].set(jnp.tile(diff_qk_gain[li, 0], 8)).at[1].set(jnp.tile(diff_qk_gain[li, 1], 8))
        qkg = qkg.at[2].set(jnp.tile(gqa_qk_gain[li, 0], 4)).at[3].set(jnp.tile(gqa_qk_gain[li, 1], 4))
        qta, ka, vta, qtb, kb, vtb, U = _kin_call(
            X, modtab, norm1[li][None, :], _reorder_w_in(w_in[li]), g32, g64, qkg, cosa, sina, cosb, sinb, B, T)

        oa = _attn_call(qta, ka, vta, diff_lambda[li], B, T, CT, True, lam_init, "attn_diff")
        ob = _attn_call(qtb, kb, vtb, diff_lambda[li], B, T, CT, False, lam_init, "attn_gqa")

        bias128 = jnp.zeros((1, LANES), F32).at[0, :16].set(mlstm_gate_bias[li].reshape(-1))
        hcf = _mlstm_call(U, bias128, e_i, e_f, B, T, CT, False)
        hcb = _mlstm_call(U, bias128, e_i, e_f, B, T, CT, True)

        alog_e = jnp.repeat(gdn_a_log[li], HD, axis=1)
        dtb_e = jnp.repeat(gdn_dt_bias[li], HD, axis=1)
        qd, kd, vd, bf, gf, bb, gb = _gdn_prep_call(U, gdn_conv[li], g64, e_d, alog_e, dtb_e, B, T, CT)
        odf = _gdn_call(qd, kd, vd, bf, gf, B, T, CT, False)
        odb = _gdn_call(qd, kd, vd, bb, gb, B, T, CT, True)

        gains = jnp.zeros((8, GW), F32)
        gains = gains.at[0].set(jnp.tile(diff_subln[li], 4)).at[1].set(mlstm_norm[li])
        gains = gains.at[2].set(jnp.tile(gdn_norm[li], 4))
        X = _post_call(X, modtab, oa, ob, hcf, hcb, odf, odb, U, gains, g64, w_out[li].astype(BF16), B, T, lam_init)

        if with_ctx:
            tiles_per_b, kind = T // TM, (lambda i: jnp.minimum(i, 1))
        else:
            X = X.reshape(B, T, D)[:, CT:].reshape(B * S, D)
            nb_rows = S
            tiles_per_b, kind = S // TM, (lambda i: 1)
        j = li // 2
        if li % 2 == 0:
            X = _ffn_call(X, modtab, norm2[li][None, :], ffn_w_gate[j].astype(BF16), ffn_w_up[j].astype(BF16),
                          ffn_w_down[j].astype(BF16), B, tiles_per_b, kind)
        else:
            assert not with_ctx
            n_exp = moe_router.shape[2]
            router128 = jnp.pad(moe_router[j], ((0, 0), (0, LANES - n_exp)))
            hb, gates = _router_call(X, modtab, norm2[li][None, :], router128, B, tiles_per_b, n_exp)
            X = _moe_call(X, modtab, hb, gates, moe_w_gate[j].astype(BF16), moe_w_up[j].astype(BF16),
                          moe_w_down[j].astype(BF16), B, nb_rows)
    return X.reshape(B, nb_rows, D)[:, nb_rows - S:]
```

```python
import functools
import math

import numpy as np
import jax
import jax.numpy as jnp
from jax import lax
from jax.experimental import pallas as pl
from jax.experimental.pallas import tpu as pltpu

F32 = jnp.float32
BF16 = jnp.bfloat16
EPS = 1e-6
ROPE_THETA = 10000.0
GRID_W = 64
LOG2E = 1.4426950408889634

TM = 256
CH = 64
HD = 64
GW = 256
LANES = 128
VMEM_LIMIT = 56 * 1024 * 1024

NN = (((1,), (0,)), ((), ()))
NT = (((1,), (1,)), ((), ()))
TN = (((0,), (0,)), ((), ()))


def _dot(a, b, dims=NN):
    return lax.dot_general(a.astype(BF16), b.astype(BF16), dims, preferred_element_type=F32)


def _split(a):
    hi = a.astype(BF16)
    lo = (a - hi.astype(F32)).astype(BF16)
    return hi, lo


def _dot_xl(a, b_exact, dims=NN):
    hi, lo = _split(a)
    bb = b_exact.astype(BF16)
    return (lax.dot_general(hi, bb, dims, preferred_element_type=F32)
            + lax.dot_general(lo, bb, dims, preferred_element_type=F32))


def _dot_xr(a_exact, b, dims=NN):
    hi, lo = _split(b)
    aa = a_exact.astype(BF16)
    return (lax.dot_general(aa, hi, dims, preferred_element_type=F32)
            + lax.dot_general(aa, lo, dims, preferred_element_type=F32))


def _dot_x3(a, b, dims=NN):
    ah, al = _split(a)
    bh, bl = _split(b)
    d = functools.partial(lax.dot_general, dimension_numbers=dims, preferred_element_type=F32)
    return d(ah, bh) + d(ah, bl) + d(al, bh)


def _sigmoid(x):
    return 1.0 / (1.0 + jnp.exp(-x))


def _silu(x):
    return x * _sigmoid(x)


def _softplus(x):
    return jnp.maximum(x, 0.0) + jnp.log(1.0 + jnp.exp(-jnp.abs(x)))


def _cparams(sem, vmem=VMEM_LIMIT):
    return pltpu.CompilerParams(dimension_semantics=sem, vmem_limit_bytes=vmem)


def _mod_kernel(c_ref, w_ref, b_ref, o_ref):
    cv = c_ref[...]
    s = _silu(cv)
    o_ref[0] = lax.dot_general(s, w_ref[0], NN, precision=lax.Precision.HIGHEST,
                               preferred_element_type=F32) + b_ref[0]


def _mod_call(cvec, w_mod, b_mod):
    depth, d, n6 = w_mod.shape
    rows = cvec.shape[0]
    tn = 1536
    return pl.pallas_call(
        _mod_kernel,
        grid=(depth, n6 // tn),
        in_specs=[pl.BlockSpec((rows, d), lambda l, j: (0, 0)),
                  pl.BlockSpec((1, d, tn), lambda l, j: (l, 0, j)),
                  pl.BlockSpec((1, 1, tn), lambda l, j: (l, 0, j))],
        out_specs=pl.BlockSpec((1, rows, tn), lambda l, j: (l, 0, j)),
        out_shape=jax.ShapeDtypeStruct((depth, rows, n6), F32),
        compiler_params=_cparams(("arbitrary", "arbitrary")),
        name="adaln_mod",
    )(cvec, w_mod, b_mod.reshape(depth, 1, n6))


def _norm_mod(x, gain, shift, scale):
    ms = jnp.mean(x * x, axis=-1, keepdims=True)
    return (x * lax.rsqrt(ms + EPS) * gain) * (1.0 + scale) + shift


def _group_mean(v, g_ref):
    return _dot_xl(v, g_ref[...])


def _rope(xn, cos, sin, half):
    w = xn.shape[1]
    lane = lax.broadcasted_iota(jnp.int32, xn.shape, 1)
    first = (lane % (2 * half)) < half
    xs = jnp.where(first, pltpu.roll(xn, w - half, 1), pltpu.roll(xn, half, 1))
    return xn * cos + xs * sin


A_W = 3 * GW
B_W = GW + 2 * 128
AB_W = A_W + B_W
U_W = 2176
U_DQKV, U_CQ, U_CK, U_CV, U_CO, U_DZ, U_G = 0, 768, 1024, 1280, 1536, 1792, 2048


def _kin_kernel(x_ref, mod_ref, gain_ref, w_ref, g32_ref, g64_ref, qkg_ref,
                cosa_ref, sina_ref, cosb_ref, sinb_ref,
                qta_ref, ka_ref, vta_ref, qtb_ref, kb_ref, vtb_ref, u_ref):
    x = x_ref[...]
    mod = mod_ref[0, 0]
    h = _norm_mod(x, gain_ref[...], mod[0:1], mod[1:2]).astype(BF16)

    ya = lax.dot_general(h, w_ref[:, 0:A_W], NN, preferred_element_type=F32)
    qkg = qkg_ref[...]
    ca, sa = cosa_ref[...], sina_ref[...]

    def prep(v, g_ref, gain, cos, sin, half):
        vn = v * lax.rsqrt(_group_mean(v * v, g_ref) + EPS) * gain
        return _rope(vn, cos, sin, half)

    qa = prep(ya[:, 0:GW], g32_ref, qkg[0:1], ca, sa, 8) * ((HD // 2) ** -0.5 * LOG2E)
    ka = prep(ya[:, GW:2 * GW], g32_ref, qkg[1:2], ca, sa, 8)
    qta_ref[0, 0] = qa.T.astype(BF16)
    ka_ref[...] = ka.astype(BF16)
    vta_ref[0, 0] = ya[:, 2 * GW:3 * GW].T.astype(BF16)

    yb = lax.dot_general(h, w_ref[:, A_W:AB_W], NN, preferred_element_type=F32)
    cb, sb = cosb_ref[...], sinb_ref[...]
    qb = prep(yb[:, 0:GW], g64_ref, qkg[2:3], cb, sb, 16) * (HD ** -0.5 * LOG2E)
    kb = prep(yb[:, GW:GW + 128], g64_ref.at[0:128, 0:128], qkg[3:4, 0:128], cb[:, 0:128], sb[:, 0:128], 16)
    qtb_ref[0, 0] = qb.T.astype(BF16)
    kb_ref[...] = kb.astype(BF16)
    vtb_ref[0, 0] = yb[:, GW + 128:GW + 256].T.astype(BF16)

    u_ref[...] = lax.dot_general(h, w_ref[:, AB_W:AB_W + U_W], NN, preferred_element_type=F32)


def _kin_call(X, modtab, gain, w_r, g32, g64, qkg, cosa, sina, cosb, sinb, B, T):
    nt = T // TM
    d = X.shape[1]
    wcols = w_r.shape[1]
    row = lambda b, i: (b * nt + i, 0)
    const = lambda b, i: (0, 0)
    pos = lambda b, i: (i, 0)
    tmap = lambda b, i: (b, i, 0, 0)
    in_specs = [
        pl.BlockSpec((TM, d), row),
        pl.BlockSpec((1, 1, 8, d), lambda b, i: (b, jnp.minimum(i, 1), 0, 0)),
        pl.BlockSpec((1, d), const),
        pl.BlockSpec((d, wcols), const),
        pl.BlockSpec((GW, GW), const),
        pl.BlockSpec((GW, GW), const),
        pl.BlockSpec((8, GW), const),
        pl.BlockSpec((TM, GW), pos), pl.BlockSpec((TM, GW), pos),
        pl.BlockSpec((TM, GW), pos), pl.BlockSpec((TM, GW), pos),
    ]
    out_specs = [
        pl.BlockSpec((1, 1, GW, TM), tmap), pl.BlockSpec((TM, GW), row), pl.BlockSpec((1, 1, GW, TM), tmap),
        pl.BlockSpec((1, 1, GW, TM), tmap), pl.BlockSpec((TM, 128), row), pl.BlockSpec((1, 1, 128, TM), tmap),
        pl.BlockSpec((TM, U_W), row),
    ]
    out_shape = [
        jax.ShapeDtypeStruct((B, nt, GW, TM), BF16), jax.ShapeDtypeStruct((B * T, GW), BF16),
        jax.ShapeDtypeStruct((B, nt, GW, TM), BF16),
        jax.ShapeDtypeStruct((B, nt, GW, TM), BF16), jax.ShapeDtypeStruct((B * T, 128), BF16),
        jax.ShapeDtypeStruct((B, nt, 128, TM), BF16),
        jax.ShapeDtypeStruct((B * T, U_W), F32),
    ]
    return pl.pallas_call(
        _kin_kernel, grid=(B, nt), in_specs=in_specs, out_specs=out_specs, out_shape=out_shape,
        compiler_params=_cparams(("arbitrary", "arbitrary")), name="in_proj",
    )(X, modtab, gain, w_r, g32, g64, qkg, cosa, sina, cosb, sinb)


def _attn_kernel(qt_ref, k_ref, vt_ref, lam_ref, o_ref, *, diff, lam_init, n_ctx_tiles, n_ctx_blk, n_all_blk):
    i = pl.program_id(1)
    nkv = jnp.where(i < n_ctx_tiles, n_ctx_blk, n_all_blk)
    qt = qt_ref[0, 0]
    kd = k_ref.shape[1]
    rowi = lax.broadcasted_iota(jnp.int32, (kd, TM), 0)

    def run_streams(qms, vlos):
        def body(j, carry):
            off = pl.multiple_of(j * TM, TM)
            kb = k_ref[pl.ds(off, TM), :]
            ss = [lax.dot_general(kb, qm, NN, preferred_element_type=F32) for qm in qms]
            stats = []
            for (m, l, _), s in zip(carry, ss):
                m_new = jnp.maximum(m, jnp.max(s, axis=0, keepdims=True))
                alpha = jnp.exp2(m - m_new)
                p = jnp.exp2(s - m_new)
                stats.append((m_new, alpha, alpha * l + jnp.sum(p, axis=0, keepdims=True), p.astype(BF16)))
            new = []
            for (_, _, acc), (m_new, alpha, l, p), vlo in zip(carry, stats, vlos):
                vb = vt_ref[0, j, vlo:vlo + HD, :]
                acc = alpha * acc + lax.dot_general(vb, p, NN, preferred_element_type=F32)
                new.append((m_new, l, acc))
            return tuple(new)

        m0 = jnp.full((1, TM), -jnp.inf, F32)
        l0 = jnp.zeros((1, TM), F32)
        a0 = jnp.zeros((HD, TM), F32)
        res = lax.fori_loop(0, nkv, body, tuple((m0, l0, a0) for _ in qms))
        return [acc * (1.0 / l) for (_, l, acc) in res]

    outs = []
    if diff:
        lv = lam_ref[...]
        lam = (jnp.exp(jnp.sum(lv[0:1] * lv[1:2], axis=1, keepdims=True))
               - jnp.exp(jnp.sum(lv[2:3] * lv[3:4], axis=1, keepdims=True)) + lam_init)
        for hp in range(2):
            qms, vlos = [], []
            for h in (2 * hp, 2 * hp + 1):
                for j in range(2):
                    lo = (2 * h + j) * (HD // 2)
                    qms.append(jnp.where((rowi >= lo) & (rowi < lo + HD // 2), qt, jnp.zeros_like(qt)))
                    vlos.append(h * HD)
            r = run_streams(qms, vlos)
            outs += [r[0] - lam * r[1], r[2] - lam * r[3]]
    else:
        qms, vlos = [], []
        for h in range(4):
            kvh = h // 2
            qh = qt[h * HD:(h + 1) * HD, :]
            z = jnp.zeros_like(qh)
            qms.append(jnp.concatenate([qh, z] if kvh == 0 else [z, qh], axis=0))
            vlos.append(kvh * HD)
        outs = run_streams(qms, vlos)
    o_ref[...] = jnp.concatenate(outs, axis=0).T


def _attn_call(qt, k, vt, lam, B, T, CT, diff, lam_init, name):
    nt = T // TM
    kd = k.shape[1]
    vr = vt.shape[2]
    kern = functools.partial(_attn_kernel, diff=diff, lam_init=lam_init, n_ctx_tiles=CT // TM,
                             n_ctx_blk=CT // TM, n_all_blk=nt)
    return pl.pallas_call(
        kern, grid=(B, nt),
        in_specs=[pl.BlockSpec((1, 1, GW, TM), lambda b, i: (b, i, 0, 0)),
                  pl.BlockSpec((T, kd), lambda b, i: (b, 0)),
                  pl.BlockSpec((1, nt, vr, TM), lambda b, i: (b, 0, 0, 0)),
                  pl.BlockSpec(lam.shape, lambda b, i: (0, 0))],
        out_specs=pl.BlockSpec((TM, GW), lambda b, i: (b * nt + i, 0)),
        out_shape=jax.ShapeDtypeStruct((B * T, GW), F32),
        compiler_params=_cparams(("arbitrary", "arbitrary")), name=name,
    )(qt, k, vt, lam)


def _chunk_masks(reverse):
    row = lax.broadcasted_iota(jnp.int32, (CH, GW), 0)
    lane = lax.broadcasted_iota(jnp.int32, (CH, GW), 1)
    sidx = lane % CH
    eye = row == sidx
    causal = (sidx >= row) if reverse else (sidx <= row)
    strict = (sidx > row) if reverse else (sidx < row)
    r2 = lax.broadcasted_iota(jnp.int32, (GW, GW), 0)
    c2 = lax.broadcasted_iota(jnp.int32, (GW, GW), 1)
    blk = (r2 // CH) == (c2 // CH)
    t1 = lax.broadcasted_iota(jnp.int32, (CH, CH), 0)
    t2 = lax.broadcasted_iota(jnp.int32, (CH, CH), 1)
    tri = ((t2 >= t1) if reverse else (t2 <= t1)).astype(F32)
    return eye, causal, strict, blk, tri


def _bd(y, blk):
    return jnp.where(blk, jnp.concatenate([y, y, y, y], axis=0), 0.0)


def _row_form(col, eye):
    return _dot_xr(jnp.ones((CH, CH), F32), jnp.where(eye, col, 0.0))


def _cummax(a, reverse):
    n = a.shape[0]
    row = lax.broadcasted_iota(jnp.int32, a.shape, 0)
    k = 1
    while k < n:
        if reverse:
            sh, valid = pltpu.roll(a, n - k, 0), row < n - k
        else:
            sh, valid = pltpu.roll(a, k, 0), row >= k
        a = jnp.maximum(a, jnp.where(valid, sh, -jnp.inf))
        k *= 2
    return a


def _chunk_order(p, nc_ctx, nch, reverse):
    if not reverse:
        return p
    return jnp.where(p < nc_ctx, nc_ctx - 1 - p, nch + nc_ctx - 1 - p)


def _mlstm_kernel(q_ref, k_ref, v_ref, g_ref, bias_ref, ei_ref, ef_ref, h_ref, c_sc, n_sc, m_sc, *, reverse):
    @pl.when(pl.program_id(0) == 0)
    def _():
        c_sc[...] = jnp.zeros_like(c_sc)
        n_sc[...] = jnp.zeros_like(n_sc)
        m_sc[...] = jnp.zeros_like(m_sc)

    eye, causal, _, blk, tri = _chunk_masks(reverse)
    last = 0 if reverse else CH - 1
    R = range(q_ref.shape[0])
    q4 = [q_ref[b] for b in R]
    k4 = [k_ref[b] * (HD ** -0.5) for b in R]
    v4 = [v_ref[b] for b in R]
    gt = [g_ref[b] + bias_ref[...] for b in R]
    i_e = [_dot_xl(gt[b], ei_ref[0]) for b in R]
    f_e = [_dot_xl(gt[b], ef_ref[0]) for b in R]
    logf = [jnp.minimum(f_e[b], 0.0) - jnp.log(1.0 + jnp.exp(-jnp.abs(f_e[b]))) for b in R]
    b_e = [_dot_xr(tri, logf[b]) for b in R]
    m_prev = [m_sc[b, 0:1, :] for b in R]
    a_e = [i_e[b] - b_e[b] for b in R]
    rowa = [_row_form(a_e[b], eye) for b in R]
    inter = [b_e[b] + m_prev[b] for b in R]
    m_t = [jnp.maximum(inter[b], b_e[b] + _cummax(a_e[b], reverse)) for b in R]
    w_inter = [jnp.exp(inter[b] - m_t[b]) for b in R]
    dexp = [jnp.exp(jnp.where(causal, b_e[b] - m_t[b] + rowa[b], -jnp.inf)) for b in R]
    smat = [_dot(q4[b], _bd(k4[b], blk), NT) * dexp[b] for b in R]
    cbd = [c_sc[b] for b in R]
    nbd = [n_sc[b] for b in R]
    qc = [_dot(q4[b], cbd[b]) for b in R]
    qn = [_dot(q4[b], nbd[b]) for b in R]
    sv = [_dot(smat[b], _bd(v4[b], blk)) for b in R]
    ssum = [_dot_xl(smat[b], blk.astype(F32)) for b in R]
    b_last = [b_e[b][last:last + 1, :] for b in R]
    g = [b_last[b] - b_e[b] + i_e[b] for b in R]
    m_new = [jnp.maximum(b_last[b] + m_prev[b], jnp.max(g[b], axis=0, keepdims=True)) for b in R]
    ks = [k4[b] * jnp.exp(g[b] - m_new[b]) for b in R]
    cup = [_dot(ks[b], v4[b], TN) for b in R]
    nup = [_dot(ks[b], jnp.ones((CH, GW), F32), TN) for b in R]
    for b in R:
        num = w_inter[b] * qc[b] + sv[b]
        den = w_inter[b] * qn[b] + ssum[b]
        h_ref[b] = num / jnp.maximum(jnp.abs(den), jnp.exp(-m_t[b]))
        a_prev = jnp.exp(b_last[b] + m_prev[b] - m_new[b])
        c_sc[b] = a_prev * cbd[b] + jnp.where(blk, cup[b], 0.0)
        n_sc[b] = a_prev * nbd[b] + jnp.where(blk, nup[b], 0.0)
        m_sc[b] = jnp.broadcast_to(m_new[b], (8, GW))


def _mlstm_call(U, bias128, e_i, e_f, B, T, CT, reverse):
    nch, ncc = T // CH, CT // CH
    U3 = U.reshape(B, T, U.shape[1])
    cmap = lambda col: (lambda p: (0, _chunk_order(p, ncc, nch, reverse), col))
    d = 1 if reverse else 0
    out = pl.pallas_call(
        functools.partial(_mlstm_kernel, reverse=reverse), grid=(nch,),
        in_specs=[pl.BlockSpec((B, CH, GW), cmap(U_CQ // GW)),
                  pl.BlockSpec((B, CH, GW), cmap(U_CK // GW)),
                  pl.BlockSpec((B, CH, GW), cmap(U_CV // GW)),
                  pl.BlockSpec((B, CH, LANES), cmap(U_G // LANES)),
                  pl.BlockSpec((1, LANES), lambda p: (0, 0)),
                  pl.BlockSpec((1, LANES, GW), lambda p: (d, 0, 0)),
                  pl.BlockSpec((1, LANES, GW), lambda p: (d, 0, 0))],
        out_specs=pl.BlockSpec((B, CH, GW), cmap(0)),
        out_shape=jax.ShapeDtypeStruct((B, T, GW), F32),
        scratch_shapes=[pltpu.VMEM((B, GW, GW), F32), pltpu.VMEM((B, GW, GW), F32), pltpu.VMEM((B, 8, GW), F32)],
        compiler_params=_cparams(("arbitrary",)),
        name="mlstm_bwd" if reverse else "mlstm_fwd",
    )(U3, U3, U3, U3, bias128, e_i, e_f)
    return out.reshape(B * T, GW)


def _gdn_prep_kernel(prev_ref, cur_ref, next_ref, g_ref, conv_ref, g64_ref, e_ref, alog_ref, dtb_ref,
                     q_ref, k_ref, v_ref, bf_ref, gf_ref, bb_ref, gb_ref, *, n_ctx_tiles, nt):
    i = pl.program_id(1)
    at_start = (i == 0) | (i == n_ctx_tiles)
    at_end = (i == n_ctx_tiles - 1) | (i == nt - 1)
    cur = cur_ref[...]
    head = jnp.where(at_start, 0.0, prev_ref[TM - 8:TM, :])
    tail = jnp.where(at_end, 0.0, next_ref[0:8, :])
    xx = jnp.concatenate([head, cur, tail], axis=0)
    cw = conv_ref[...]
    kw = cw.shape[0]
    acc = jnp.zeros_like(cur)
    for j in range(kw):
        s0 = 8 + j - kw // 2
        acc = acc + xx[s0:s0 + TM, :] * cw[j:j + 1, :]
    t = _silu(acc)
    q, k, v = t[:, 0:GW], t[:, GW:2 * GW], t[:, 2 * GW:3 * GW]
    g64 = g64_ref[...] * float(HD)
    q_ref[...] = q * lax.rsqrt(_dot_xl(q * q, g64) + EPS) * (HD ** -0.5)
    k_ref[...] = k * lax.rsqrt(_dot_xl(k * k, g64) + EPS)
    v_ref[...] = v
    gt = g_ref[...]
    for d, (b_out, g_out) in enumerate(((bf_ref, gf_ref), (bb_ref, gb_ref))):
        b_out[...] = _sigmoid(_dot_xl(gt, e_ref[2 * d]))
        a_e = _dot_xl(gt, e_ref[2 * d + 1])
        g_out[...] = -jnp.exp(alog_ref[d:d + 1, :]) * _softplus(a_e + dtb_ref[d:d + 1, :])


def _gdn_prep_call(U, conv_w, g64, e_d, alog_e, dtb_e, B, T, CT):
    nt = T // TM
    w3 = 3 * GW
    row = lambda b, i: (b * nt + i, 0)
    const2 = lambda b, i: (0, 0)
    o = jax.ShapeDtypeStruct((B * T, GW), F32)
    return pl.pallas_call(
        functools.partial(_gdn_prep_kernel, n_ctx_tiles=CT // TM, nt=nt), grid=(B, nt),
        in_specs=[pl.BlockSpec((TM, w3), lambda b, i: (b * nt + jnp.maximum(i - 1, 0), 0)),
                  pl.BlockSpec((TM, w3), row),
                  pl.BlockSpec((TM, w3), lambda b, i: (b * nt + jnp.minimum(i + 1, nt - 1), 0)),
                  pl.BlockSpec((TM, LANES), lambda b, i: (b * nt + i, U_G // LANES)),
                  pl.BlockSpec(conv_w.shape, const2),
                  pl.BlockSpec((GW, GW), const2),
                  pl.BlockSpec(e_d.shape, lambda b, i: (0, 0, 0)),
                  pl.BlockSpec((2, GW), const2), pl.BlockSpec((2, GW), const2)],
        out_specs=[pl.BlockSpec((TM, GW), row)] * 7,
        out_shape=[o] * 7,
        compiler_params=_cparams(("arbitrary", "arbitrary")), name="gdn_prep",
    )(U, U, U, U, conv_w, g64, e_d, alog_e, dtb_e)


def _gdn_kernel(q_ref, k_ref, v_ref, beta_ref, g_ref, o_ref, s_sc, *, reverse):
    @pl.when(pl.program_id(0) == 0)
    def _():
        s_sc[...] = jnp.zeros_like(s_sc)

    eye, causal, strict, blk, tri = _chunk_masks(reverse)
    last = 0 if reverse else CH - 1
    R = range(q_ref.shape[0])
    q4, k4, v4, beta = ([r[b] for b in R] for r in (q_ref, k_ref, v_ref, beta_ref))
    gam = [_dot_xr(tri, g_ref[b]) for b in R]
    rowg = [_row_form(gam[b], eye) for b in R]
    decay = [jnp.exp(jnp.where(causal, gam[b] - rowg[b], -jnp.inf)) for b in R]
    kbeta = [k4[b] * beta[b] for b in R]
    kbd = [_bd(k4[b], blk) for b in R]
    a4 = [jnp.where(strict, _dot(kbeta[b], kbd[b], NT) * decay[b], 0.0) for b in R]
    attn = [_dot(q4[b], kbd[b], NT) * decay[b] for b in R]
    tinv = [jnp.where(eye, 1.0, 0.0) - a4[b] for b in R]
    apow = a4
    for _ in range(5):
        apow = [_dot_x3(apow[b], _bd(apow[b], blk)) for b in R]
        tinv = [tinv[b] + _dot_x3(tinv[b], _bd(apow[b], blk)) for b in R]
    u4 = [_dot_x3(tinv[b], _bd(v4[b] * beta[b], blk)) for b in R]
    w4 = [_dot_x3(tinv[b], _bd(kbeta[b] * jnp.exp(gam[b]), blk)) for b in R]
    sbd = [s_sc[b] for b in R]
    v_new = [u4[b] - _dot(w4[b], sbd[b]) for b in R]
    o_inter = [_dot(q4[b] * jnp.exp(gam[b]), sbd[b]) for b in R]
    o_intra = [_dot(attn[b], _bd(v_new[b], blk)) for b in R]
    upd = [_dot(k4[b] * jnp.exp(gam[b][last:last + 1, :] - gam[b]), v_new[b], TN) for b in R]
    for b in R:
        o_ref[b] = o_inter[b] + o_intra[b]
        s_sc[b] = sbd[b] * jnp.exp(gam[b][last:last + 1, :]) + jnp.where(blk, upd[b], 0.0)


def _gdn_call(q, k, v, beta, g, B, T, CT, reverse):
    nch, ncc = T // CH, CT // CH
    spec = pl.BlockSpec((B, CH, GW), lambda p: (0, _chunk_order(p, ncc, nch, reverse), 0))
    r3 = lambda a: a.reshape(B, T, GW)
    out = pl.pallas_call(
        functools.partial(_gdn_kernel, reverse=reverse), grid=(nch,),
        in_specs=[spec] * 5, out_specs=spec,
        out_shape=jax.ShapeDtypeStruct((B, T, GW), F32),
        scratch_shapes=[pltpu.VMEM((B, GW, GW), F32)],
        compiler_params=_cparams(("arbitrary",)),
        name="gdn_bwd" if reverse else "gdn_fwd",
    )(r3(q), r3(k), r3(v), r3(beta), r3(g))
    return out.reshape(B * T, GW)


def _post_kernel(x_ref, mod_ref, oa_ref, ob_ref, hcf_ref, hcb_ref, odf_ref, odb_ref, oc_ref, zd_ref,
                 gains_ref, g64_ref, w_ref, xo_ref, *, lam_init):
    gains = gains_ref[...]
    g64 = g64_ref

    oa = oa_ref[...]
    ya = oa * lax.rsqrt(_group_mean(oa * oa, g64) + EPS) * gains[0:1] * (1.0 - lam_init)
    yb = ob_ref[...]
    hc = hcf_ref[...] + hcb_ref[...]
    dc = hc - _group_mean(hc, g64)
    yc = _sigmoid(oc_ref[...]) * (dc * lax.rsqrt(_group_mean(dc * dc, g64) + EPS) * gains[1:2])
    od = odf_ref[...] + odb_ref[...]
    yd = od * lax.rsqrt(_group_mean(od * od, g64) + EPS) * gains[2:3] * _silu(zd_ref[...])
    out = _dot(ya, w_ref[0:GW, :])
    out = out + _dot(yb, w_ref[GW:2 * GW, :])
    out = out + _dot(yc, w_ref[2 * GW:3 * GW, :])
    out = out + _dot(yd, w_ref[3 * GW:4 * GW, :])
    xo_ref[...] = x_ref[...] + mod_ref[0, 0][2:3] * out


def _post_call(X, modtab, oa, ob, hcf, hcb, odf, odb, U, gains, g64, w_out, B, T, lam_init):
    nt = T // TM
    d = X.shape[1]
    row = lambda b, i: (b * nt + i, 0)
    const = lambda b, i: (0, 0)
    gspec = pl.BlockSpec((TM, GW), row)
    return pl.pallas_call(
        functools.partial(_post_kernel, lam_init=lam_init), grid=(B, nt),
        in_specs=[pl.BlockSpec((TM, d), row),
                  pl.BlockSpec((1, 1, 8, d), lambda b, i: (b, jnp.minimum(i, 1), 0, 0)),
                  gspec, gspec, gspec, gspec, gspec, gspec,
                  pl.BlockSpec((TM, GW), lambda b, i: (b * nt + i, U_CO // GW)),
                  pl.BlockSpec((TM, GW), lambda b, i: (b * nt + i, U_DZ // GW)),
                  pl.BlockSpec((8, GW), const), pl.BlockSpec((GW, GW), const),
                  pl.BlockSpec(w_out.shape, const)],
        out_specs=pl.BlockSpec((TM, d), row),
        out_shape=jax.ShapeDtypeStruct(X.shape, F32),
        compiler_params=_cparams(("arbitrary", "arbitrary")), name="out_proj",
    )(X, modtab, oa, ob, hcf, hcb, odf, odb, U, U, gains, g64, w_out)


FF_CHUNK = 256


def _ffn_kernel(x_ref, mod_ref, gain_ref, wg_ref, wu_ref, wd_ref, xo_ref):
    x = x_ref[...]
    mod = mod_ref[0, 0]
    h = _norm_mod(x, gain_ref[...], mod[3:4], mod[4:5]).astype(BF16)
    acc = jnp.zeros(x.shape, F32)
    for f in range(0, wg_ref.shape[1], FF_CHUNK):
        g = lax.dot_general(h, wg_ref[:, f:f + FF_CHUNK], NN, preferred_element_type=F32)
        u = lax.dot_general(h, wu_ref[:, f:f + FF_CHUNK], NN, preferred_element_type=F32)
        acc = acc + _dot(_silu(g) * u, wd_ref[f:f + FF_CHUNK, :])
    xo_ref[...] = x + mod[5:6] * acc


def _ffn_call(X, modtab, gain, wg, wu, wd, nb, tiles_per_b, kind_of_tile):
    d = X.shape[1]
    row = lambda b, i: (b * tiles_per_b + i, 0)
    const = lambda b, i: (0, 0)
    return pl.pallas_call(
        _ffn_kernel, grid=(nb, tiles_per_b),
        in_specs=[pl.BlockSpec((TM, d), row),
                  pl.BlockSpec((1, 1, 8, d), lambda b, i: (b, kind_of_tile(i), 0, 0)),
                  pl.BlockSpec((1, d), const),
                  pl.BlockSpec(wg.shape, const), pl.BlockSpec(wu.shape, const), pl.BlockSpec(wd.shape, const)],
        out_specs=pl.BlockSpec((TM, d), row),
        out_shape=jax.ShapeDtypeStruct(X.shape, F32),
        compiler_params=_cparams(("arbitrary", "arbitrary")), name="ffn_dense",
    )(X, modtab, gain, wg, wu, wd)


def _router_kernel(x_ref, mod_ref, gain_ref, r_ref, h_ref, gates_ref, *, n_exp):
    x = x_ref[...]
    mod = mod_ref[0, 0]
    h = _norm_mod(x, gain_ref[...], mod[3:4], mod[4:5])
    h_ref[...] = h.astype(BF16)
    logits = lax.dot_general(h, r_ref[...], NN, precision=lax.Precision.HIGHEST, preferred_element_type=F32)
    lane = lax.broadcasted_iota(jnp.int32, logits.shape, 1)
    neg = -jnp.inf
    lg = jnp.where(lane < n_exp, logits, neg)
    v1 = jnp.max(lg, axis=1, keepdims=True)
    i1 = jnp.min(jnp.where(lg == v1, lane, LANES), axis=1, keepdims=True)
    lg2 = jnp.where(lane == i1, neg, lg)
    v2 = jnp.max(lg2, axis=1, keepdims=True)
    i2 = jnp.min(jnp.where(lg2 == v2, lane, LANES), axis=1, keepdims=True)
    e2 = jnp.exp(v2 - v1)
    w1 = 1.0 / (1.0 + e2)
    w2 = e2 / (1.0 + e2)
    gates_ref[...] = jnp.where(lane == i1, w1, 0.0) + jnp.where(lane == i2, w2, 0.0)


def _router_call(X, modtab, gain, router128, nb, tiles_per_b, n_exp):
    d = X.shape[1]
    row = lambda b, i: (b * tiles_per_b + i, 0)
    const = lambda b, i: (0, 0)
    return pl.pallas_call(
        functools.partial(_router_kernel, n_exp=n_exp), grid=(nb, tiles_per_b),
        in_specs=[pl.BlockSpec((TM, d), row),
                  pl.BlockSpec((1, 1, 8, d), lambda b, i: (b, 1, 0, 0)),
                  pl.BlockSpec((1, d), const),
                  pl.BlockSpec(router128.shape, const)],
        out_specs=[pl.BlockSpec((TM, d), row), pl.BlockSpec((TM, LANES), row)],
        out_shape=[jax.ShapeDtypeStruct(X.shape, BF16), jax.ShapeDtypeStruct((X.shape[0], LANES), F32)],
        compiler_params=_cparams(("arbitrary", "arbitrary")), name="moe_router",
    )(X, modtab, gain, router128)


MOE_TM = 1024
MOE_TF = 512


def _moe_kernel(x_ref, mod_ref, h_ref, gates_ref, wg_ref, wu_ref, wd_ref, xo_ref, acc_sc):
    e = pl.program_id(2)
    f = pl.program_id(3)
    ne = pl.num_programs(2)
    nf = pl.num_programs(3)

    @pl.when((e == 0) & (f == 0))
    def _():
        acc_sc[...] = jnp.zeros_like(acc_sc)

    h = h_ref[...]
    gates = gates_ref[...]
    lane = lax.broadcasted_iota(jnp.int32, gates.shape, 1)
    gate = jnp.sum(jnp.where(lane == e, gates, 0.0), axis=1, keepdims=True)
    g = lax.dot_general(h, wg_ref[0], NN, preferred_element_type=F32)
    u = lax.dot_general(h, wu_ref[0], NN, preferred_element_type=F32)
    acc_sc[...] += _dot(_silu(g) * u * gate, wd_ref[0])

    @pl.when((e == ne - 1) & (f == nf - 1))
    def _():
        xo_ref[...] = x_ref[...] + mod_ref[0, 0][5:6] * acc_sc[...]


def _moe_call(X, modtab, hb, gates, wg, wu, wd, nb, rows_per_b):
    d = X.shape[1]
    n_exp, _, ff = wg.shape
    tm = min(MOE_TM, rows_per_b)
    tpb = rows_per_b // tm
    row = lambda b, i, e, f: (b * tpb + i, 0)
    return pl.pallas_call(
        _moe_kernel, grid=(nb, tpb, n_exp, ff // MOE_TF),
        in_specs=[pl.BlockSpec((tm, d), row),
                  pl.BlockSpec((1, 1, 8, d), lambda b, i, e, f: (b, 1, 0, 0)),
                  pl.BlockSpec((tm, d), row),
                  pl.BlockSpec((tm, LANES), row),
                  pl.BlockSpec((1, d, MOE_TF), lambda b, i, e, f: (e, 0, f)),
                  pl.BlockSpec((1, d, MOE_TF), lambda b, i, e, f: (e, 0, f)),
                  pl.BlockSpec((1, MOE_TF, d), lambda b, i, e, f: (e, f, 0))],
        out_specs=pl.BlockSpec((tm, d), row),
        out_shape=jax.ShapeDtypeStruct(X.shape, F32),
        scratch_shapes=[pltpu.VMEM((tm, d), F32)],
        compiler_params=_cparams(("arbitrary",) * 4), name="moe_experts",
    )(X, modtab, hb, gates, wg, wu, wd)


def _group_mean_matrix(group):
    idx = np.arange(GW)
    return jnp.asarray((idx[:, None] // group == idx[None, :] // group).astype(np.float32) / group, BF16)


def _rope_tables(n_lat, n_ctx, dim, reps):
    t = jnp.arange(n_lat)
    pos = jnp.stack([(t // GRID_W).astype(F32), (t % GRID_W).astype(F32)], axis=1)
    n_freq = dim // 4
    inv_freq = ROPE_THETA ** (-jnp.arange(n_freq, dtype=F32) / n_freq)
    ang = pos[:, :, None] * inv_freq
    cos = jnp.cos(ang)
    sin = jnp.sin(ang)
    cos = jnp.stack([cos, cos], axis=2).reshape(n_lat, dim)
    sin = jnp.stack([-sin, sin], axis=2).reshape(n_lat, dim)
    cos = jnp.concatenate([jnp.ones((n_ctx, dim), F32), cos], axis=0)
    sin = jnp.concatenate([jnp.zeros((n_ctx, dim), F32), sin], axis=0)
    return jnp.tile(cos, (1, reps)), jnp.tile(sin, (1, reps))


def _expand_matrix(col0):
    m = np.zeros((LANES, GW), np.float32)
    for h in range(4):
        m[col0 + h, h * HD:(h + 1) * HD] = 1.0
    return m


_SPLITS = (256, 256, 256, 256, 128, 128, 256, 256, 256, 256, 16, 768, 256, 16)


def _reorder_w_in(w):
    offs = np.concatenate([[0], np.cumsum(_SPLITS)])
    seg = [w[:, offs[i]:offs[i + 1]] for i in range(len(_SPLITS))]
    aq, ak, av, bq, bk, bv, cq, ck, cv, co, cg, dqkv, dz, dg = seg
    pad = jnp.zeros((w.shape[0], LANES - 32), w.dtype)
    return jnp.concatenate([aq, ak, av, bq, bk, bv, dqkv, cq, ck, cv, co, dz, cg, dg, pad], axis=1).astype(BF16)


def kernel(x, c, ctx, c_ctx, w_mod, b_mod, norm1, norm2, w_in, w_out, diff_qk_gain, diff_lambda, diff_subln,
           gqa_qk_gain, mlstm_gate_bias, mlstm_norm, gdn_conv, gdn_a_log, gdn_dt_bias, gdn_norm,
           ffn_w_gate, ffn_w_up, ffn_w_down, moe_router, moe_w_gate, moe_w_up, moe_w_down):
    B, S, D = x.shape
    CT = ctx.shape[1]
    T = CT + S
    depth = w_in.shape[0]
    assert CT % TM == 0 and S % TM == 0 and S % GRID_W == 0 and B + 1 <= 8

    g32 = _group_mean_matrix(HD // 2)
    g64 = _group_mean_matrix(HD)
    cosa, sina = _rope_tables(S, CT, HD // 2, 8)
    cosb, sinb = _rope_tables(S, CT, HD, 4)
    e_i = jnp.asarray(np.stack([_expand_matrix(0), _expand_matrix(8)]), F32)
    e_f = jnp.asarray(np.stack([_expand_matrix(4), _expand_matrix(12)]), F32)
    e_d = jnp.asarray(np.stack([_expand_matrix(16 + 4 * j) for j in range(4)]), F32)

    cvec = jnp.zeros((8, D), F32).at[:B].set(c).at[B].set(c_ctx)
    mod_all = _mod_call(cvec, w_mod, b_mod).reshape(depth, 8, 6, D)

    X = jnp.concatenate([ctx, x], axis=1).reshape(B * T, D)
    nb_rows = T
    for li in range(depth):
        with_ctx = li < depth - 1
        lam_init = 0.8 - 0.6 * math.exp(-0.3 * li)
        m = mod_all[li]
        modtab = jnp.stack([jnp.broadcast_to(m[B], (B, 6, D)), m[:B]], axis=1)
        modtab = jnp.pad(modtab, ((0, 0), (0, 0), (0, 2), (0, 0)))

        qkg = jnp.zeros((8, GW), F32)
        qkg = qkg.at[0].set(jnp.tile(diff_qk_gain[li, 0], 8)).at[1].set(jnp.tile(diff_qk_gain[li, 1], 8))
        qkg = qkg.at[2].set(jnp.tile(gqa_qk_gain[li, 0], 4)).at[3].set(jnp.tile(gqa_qk_gain[li, 1], 4))
        qta, ka, vta, qtb, kb, vtb, U = _kin_call(
            X, modtab, norm1[li][None, :], _reorder_w_in(w_in[li]), g32, g64, qkg, cosa, sina, cosb, sinb, B, T)

        oa = _attn_call(qta, ka, vta, diff_lambda[li], B, T, CT, True, lam_init, "attn_diff")
        ob = _attn_call(qtb, kb, vtb, diff_lambda[li], B, T, CT, False, lam_init, "attn_gqa")

        bias128 = jnp.zeros((1, LANES), F32).at[0, :16].set(mlstm_gate_bias[li].reshape(-1))
        hcf = _mlstm_call(U, bias128, e_i, e_f, B, T, CT, False)
        hcb = _mlstm_call(U, bias128, e_i, e_f, B, T, CT, True)

        alog_e = jnp.repeat(gdn_a_log[li], HD, axis=1)
        dtb_e = jnp.repeat(gdn_dt_bias[li], HD, axis=1)
        qd, kd, vd, bf, gf, bb, gb = _gdn_prep_call(U, gdn_conv[li], g64, e_d, alog_e, dtb_e, B, T, CT)
        odf = _gdn_call(qd, kd, vd, bf, gf, B, T, CT, False)
        odb = _gdn_call(qd, kd, vd, bb, gb, B, T, CT, True)

        gains = jnp.zeros((8, GW), F32)
        gains = gains.at[0].set(jnp.tile(diff_subln[li], 4)).at[1].set(mlstm_norm[li])
        gains = gains.at[2].set(jnp.tile(gdn_norm[li], 4))
        X = _post_call(X, modtab, oa, ob, hcf, hcb, odf, odb, U, gains, g64, w_out[li].astype(BF16), B, T, lam_init)

        if with_ctx:
            tiles_per_b, kind = T // TM, (lambda i: jnp.minimum(i, 1))
        else:
            X = X.reshape(B, T, D)[:, CT:].reshape(B * S, D)
            nb_rows = S
            tiles_per_b, kind = S // TM, (lambda i: 1)
        j = li // 2
        if li % 2 == 0:
            X = _ffn_call(X, modtab, norm2[li][None, :], ffn_w_gate[j].astype(BF16), ffn_w_up[j].astype(BF16),
                          ffn_w_down[j].astype(BF16), B, tiles_per_b, kind)
        else:
            assert not with_ctx
            n_exp = moe_router.shape[2]
            router128 = jnp.pad(moe_router[j], ((0, 0), (0, LANES - n_exp)))
            hb, gates = _router_call(X, modtab, norm2[li][None, :], router128, B, tiles_per_b, n_exp)
            X = _moe_call(X, modtab, hb, gates, moe_w_gate[j].astype(BF16), moe_w_up[j].astype(BF16),
                          moe_w_down[j].astype(BF16), B, nb_rows)
    return X.reshape(B, nb_rows, D)[:, nb_rows - S:]
```

```python
import functools
import math

import numpy as np
import jax
import jax.numpy as jnp
from jax import lax
from jax.experimental import pallas as pl
from jax.experimental.pallas import tpu as pltpu

F32 = jnp.float32
BF16 = jnp.bfloat16
EPS = 1e-6
ROPE_THETA = 10000.0
GRID_W = 64
LOG2E = 1.4426950408889634

TM = 256
CH = 64
HD = 64
GW = 256
LANES = 128
KV_BLKS = 2
V_ONES = 16
VMEM_LIMIT = 56 * 1024 * 1024

NN = (((1,), (0,)), ((), ()))
NT = (((1,), (1,)), ((), ()))
TN = (((0,), (0,)), ((), ()))


def _dot(a, b, dims=NN):
    return lax.dot_general(a.astype(BF16), b.astype(BF16), dims, preferred_element_type=F32)


def _split(a):
    hi = a.astype(BF16)
    lo = (a - hi.astype(F32)).astype(BF16)
    return hi, lo


def _dot_xl(a, b_exact, dims=NN):
    hi, lo = _split(a)
    bb = b_exact.astype(BF16)
    return (lax.dot_general(hi, bb, dims, preferred_element_type=F32)
            + lax.dot_general(lo, bb, dims, preferred_element_type=F32))


def _dot_xr(a_exact, b, dims=NN):
    hi, lo = _split(b)
    aa = a_exact.astype(BF16)
    return (lax.dot_general(aa, hi, dims, preferred_element_type=F32)
            + lax.dot_general(aa, lo, dims, preferred_element_type=F32))


def _dot_x3(a, b, dims=NN):
    ah, al = _split(a)
    bh, bl = _split(b)
    d = functools.partial(lax.dot_general, dimension_numbers=dims, preferred_element_type=F32)
    return d(ah, bh) + d(ah, bl) + d(al, bh)


def _sigmoid(x):
    return 1.0 / (1.0 + jnp.exp(-x))


def _silu(x):
    return x * _sigmoid(x)


def _softplus(x):
    return jnp.maximum(x, 0.0) + jnp.log(1.0 + jnp.exp(-jnp.abs(x)))


def _cparams(sem, vmem=VMEM_LIMIT):
    return pltpu.CompilerParams(dimension_semantics=sem, vmem_limit_bytes=vmem)


def _mod_kernel(c_ref, w_ref, b_ref, o_ref):
    cv = c_ref[...]
    s = _silu(cv)
    o_ref[0] = lax.dot_general(s, w_ref[0], NN, precision=lax.Precision.HIGHEST,
                               preferred_element_type=F32) + b_ref[0]


def _mod_call(cvec, w_mod, b_mod):
    depth, d, n6 = w_mod.shape
    rows = cvec.shape[0]
    tn = 1536
    return pl.pallas_call(
        _mod_kernel,
        grid=(depth, n6 // tn),
        in_specs=[pl.BlockSpec((rows, d), lambda l, j: (0, 0)),
                  pl.BlockSpec((1, d, tn), lambda l, j: (l, 0, j)),
                  pl.BlockSpec((1, 1, tn), lambda l, j: (l, 0, j))],
        out_specs=pl.BlockSpec((1, rows, tn), lambda l, j: (l, 0, j)),
        out_shape=jax.ShapeDtypeStruct((depth, rows, n6), F32),
        compiler_params=_cparams(("arbitrary", "arbitrary")),
        name="adaln_mod",
    )(cvec, w_mod, b_mod.reshape(depth, 1, n6))


def _norm_mod(x, gain, shift, scale):
    ms = jnp.mean(x * x, axis=-1, keepdims=True)
    return (x * lax.rsqrt(ms + EPS) * gain) * (1.0 + scale) + shift


def _group_mean(v, g_ref):
    return _dot_xl(v, g_ref[...])


def _rope(xn, cos, sin, half):
    w = xn.shape[1]
    lane = lax.broadcasted_iota(jnp.int32, xn.shape, 1)
    first = (lane % (2 * half)) < half
    xs = jnp.where(first, pltpu.roll(xn, w - half, 1), pltpu.roll(xn, half, 1))
    return xn * cos + xs * sin


A_W = 3 * GW
B_W = GW + 2 * 128
AB_W = A_W + B_W
U_W = 2176
U_DQKV, U_CQ, U_CK, U_CV, U_CO, U_DZ, U_G = 0, 768, 1024, 1280, 1536, 1792, 2048


def _kin_kernel(x_ref, mod_ref, gain_ref, w_ref, g32_ref, g64_ref, qkg_ref,
                cosa_ref, sina_ref, cosb_ref, sinb_ref,
                qta_ref, ka_ref, vta_ref, qtb_ref, kb_ref, vtb_ref, u_ref):
    x = x_ref[...]
    mod = mod_ref[0, 0]
    h = _norm_mod(x, gain_ref[...], mod[0:1], mod[1:2]).astype(BF16)

    ya = lax.dot_general(h, w_ref[:, 0:A_W], NN, preferred_element_type=F32)
    qkg = qkg_ref[...]
    ca, sa = cosa_ref[...], sina_ref[...]

    def prep(v, g_ref, gain, cos, sin, half):
        vn = v * lax.rsqrt(_group_mean(v * v, g_ref) + EPS) * gain
        return _rope(vn, cos, sin, half)

    qa = prep(ya[:, 0:GW], g32_ref, qkg[0:1], ca, sa, 8) * ((HD // 2) ** -0.5 * LOG2E)
    ka = prep(ya[:, GW:2 * GW], g32_ref, qkg[1:2], ca, sa, 8)
    qta_ref[0, 0] = qa.T.astype(BF16)
    ka_ref[...] = ka.astype(BF16)
    vta_ref[0, 0] = ya[:, 2 * GW:3 * GW].T.astype(BF16)

    yb = lax.dot_general(h, w_ref[:, A_W:AB_W], NN, preferred_element_type=F32)
    cb, sb = cosb_ref[...], sinb_ref[...]
    qb = prep(yb[:, 0:GW], g64_ref, qkg[2:3], cb, sb, 16) * (HD ** -0.5 * LOG2E)
    kb = prep(yb[:, GW:GW + 128], g64_ref.at[0:128, 0:128], qkg[3:4, 0:128], cb[:, 0:128], sb[:, 0:128], 16)
    qtb_ref[0, 0] = qb.T.astype(BF16)
    kb_ref[...] = kb.astype(BF16)
    vtb_ref[0, 0] = yb[:, GW + 128:GW + 256].T.astype(BF16)

    u_ref[...] = lax.dot_general(h, w_ref[:, AB_W:AB_W + U_W], NN, preferred_element_type=F32)


def _kin_call(X, modtab, gain, w_r, g32, g64, qkg, cosa, sina, cosb, sinb, B, T):
    nt = T // TM
    d = X.shape[1]
    wcols = w_r.shape[1]
    row = lambda b, i: (b * nt + i, 0)
    const = lambda b, i: (0, 0)
    pos = lambda b, i: (i, 0)
    tmap = lambda b, i: (b, i, 0, 0)
    in_specs = [
        pl.BlockSpec((TM, d), row),
        pl.BlockSpec((1, 1, 8, d), lambda b, i: (b, jnp.minimum(i, 1), 0, 0)),
        pl.BlockSpec((1, d), const),
        pl.BlockSpec((d, wcols), const),
        pl.BlockSpec((GW, GW), const),
        pl.BlockSpec((GW, GW), const),
        pl.BlockSpec((8, GW), const),
        pl.BlockSpec((TM, GW), pos), pl.BlockSpec((TM, GW), pos),
        pl.BlockSpec((TM, GW), pos), pl.BlockSpec((TM, GW), pos),
    ]
    out_specs = [
        pl.BlockSpec((1, 1, GW, TM), tmap), pl.BlockSpec((TM, GW), row), pl.BlockSpec((1, 1, GW, TM), tmap),
        pl.BlockSpec((1, 1, GW, TM), tmap), pl.BlockSpec((TM, 128), row), pl.BlockSpec((1, 1, 128, TM), tmap),
        pl.BlockSpec((TM, U_W), row),
    ]
    out_shape = [
        jax.ShapeDtypeStruct((B, nt, GW, TM), BF16), jax.ShapeDtypeStruct((B * T, GW), BF16),
        jax.ShapeDtypeStruct((B, nt, GW, TM), BF16),
        jax.ShapeDtypeStruct((B, nt, GW, TM), BF16), jax.ShapeDtypeStruct((B * T, 128), BF16),
        jax.ShapeDtypeStruct((B, nt, 128, TM), BF16),
        jax.ShapeDtypeStruct((B * T, U_W), F32),
    ]
    return pl.pallas_call(
        _kin_kernel, grid=(B, nt), in_specs=in_specs, out_specs=out_specs, out_shape=out_shape,
        compiler_params=_cparams(("arbitrary", "arbitrary")), name="in_proj",
    )(X, modtab, gain, w_r, g32, g64, qkg, cosa, sina, cosb, sinb)


def _attn_kernel(qt_ref, k_ref, vt_ref, lam_ref, o_ref, *, diff, lam_init, n_ctx_tiles, n_ctx_blk, n_all_blk):
    i = pl.program_id(1)
    n_lat_steps = jnp.where(i < n_ctx_tiles, 0, (n_all_blk - n_ctx_blk) // KV_BLKS)
    qt = qt_ref[0, 0]
    kd = k_ref.shape[1]
    rowi = lax.broadcasted_iota(jnp.int32, (kd, TM), 0)

    def run_streams(qms, vlos):
        def step(j0, carry, nblk):
            off = pl.multiple_of(j0 * TM, TM)
            kb = k_ref[pl.ds(off, nblk * TM), :]
            ss = [lax.dot_general(kb, qm, NN, preferred_element_type=F32) for qm in qms]
            stats = []
            for (m, _), s in zip(carry, ss):
                m_new = jnp.maximum(m, jnp.max(s, axis=0, keepdims=True))
                stats.append((m_new, jnp.exp2(m - m_new), jnp.exp2((s - m_new).astype(BF16))))
            ones = jnp.ones((V_ONES, nblk * TM), BF16)
            new = []
            for (_, acc), (m_new, alpha, p), vlo in zip(carry, stats, vlos):
                vb = jnp.concatenate([vt_ref[0, j0 + t, vlo:vlo + HD, :] for t in range(nblk)] , axis=1)
                acc = alpha * acc + lax.dot_general(jnp.concatenate([vb, ones], axis=0), p, NN,
                                                    preferred_element_type=F32)
                new.append((m_new, acc))
            return tuple(new)

        m0 = jnp.full((1, TM), -jnp.inf, F32)
        a0 = jnp.zeros((HD + V_ONES, TM), F32)
        res = lax.fori_loop(0, n_ctx_blk, lambda j, c: step(j, c, 1), tuple((m0, a0) for _ in qms))
        res = lax.fori_loop(0, n_lat_steps, lambda j, c: step(n_ctx_blk + j * KV_BLKS, c, KV_BLKS), res)
        return [acc[0:HD] * (1.0 / acc[HD:HD + 1]) for (_, acc) in res]

    outs = []
    if diff:
        lv = lam_ref[...]
        lam = (jnp.exp(jnp.sum(lv[0:1] * lv[1:2], axis=1, keepdims=True))
               - jnp.exp(jnp.sum(lv[2:3] * lv[3:4], axis=1, keepdims=True)) + lam_init)
        for hp in range(2):
            qms, vlos = [], []
            for h in (2 * hp, 2 * hp + 1):
                for j in range(2):
                    lo = (2 * h + j) * (HD // 2)
                    qms.append(jnp.where((rowi >= lo) & (rowi < lo + HD // 2), qt, jnp.zeros_like(qt)))
                    vlos.append(h * HD)
            r = run_streams(qms, vlos)
            outs += [r[0] - lam * r[1], r[2] - lam * r[3]]
    else:
        qms, vlos = [], []
        for h in range(4):
            kvh = h // 2
            qh = qt[h * HD:(h + 1) * HD, :]
            z = jnp.zeros_like(qh)
            qms.append(jnp.concatenate([qh, z] if kvh == 0 else [z, qh], axis=0))
            vlos.append(kvh * HD)
        outs = run_streams(qms, vlos)
    o_ref[...] = jnp.concatenate(outs, axis=0).T


def _attn_call(qt, k, vt, lam, B, T, CT, diff, lam_init, name):
    nt = T // TM
    kd = k.shape[1]
    vr = vt.shape[2]
    assert (nt - CT // TM) % KV_BLKS == 0
    kern = functools.partial(_attn_kernel, diff=diff, lam_init=lam_init, n_ctx_tiles=CT // TM,
                             n_ctx_blk=CT // TM, n_all_blk=nt)
    return pl.pallas_call(
        kern, grid=(B, nt),
        in_specs=[pl.BlockSpec((1, 1, GW, TM), lambda b, i: (b, i, 0, 0)),
                  pl.BlockSpec((T, kd), lambda b, i: (b, 0)),
                  pl.BlockSpec((1, nt, vr, TM), lambda b, i: (b, 0, 0, 0)),
                  pl.BlockSpec(lam.shape, lambda b, i: (0, 0))],
        out_specs=pl.BlockSpec((TM, GW), lambda b, i: (b * nt + i, 0)),
        out_shape=jax.ShapeDtypeStruct((B * T, GW), F32),
        compiler_params=_cparams(("arbitrary", "arbitrary")), name=name,
    )(qt, k, vt, lam)


def _chunk_masks(reverse):
    row = lax.broadcasted_iota(jnp.int32, (CH, GW), 0)
    lane = lax.broadcasted_iota(jnp.int32, (CH, GW), 1)
    sidx = lane % CH
    eye = row == sidx
    causal = (sidx >= row) if reverse else (sidx <= row)
    strict = (sidx > row) if reverse else (sidx < row)
    r2 = lax.broadcasted_iota(jnp.int32, (GW, GW), 0)
    c2 = lax.broadcasted_iota(jnp.int32, (GW, GW), 1)
    blk = (r2 // CH) == (c2 // CH)
    t1 = lax.broadcasted_iota(jnp.int32, (CH, CH), 0)
    t2 = lax.broadcasted_iota(jnp.int32, (CH, CH), 1)
    tri = ((t2 >= t1) if reverse else (t2 <= t1)).astype(F32)
    return eye, causal, strict, blk, tri


def _bd(y, blk):
    return jnp.where(blk, jnp.concatenate([y, y, y, y], axis=0), 0.0)


def _row_form(col, eye):
    return _dot_xr(jnp.ones((CH, CH), F32), jnp.where(eye, col, 0.0))


def _cummax(a, reverse):
    n = a.shape[0]
    row = lax.broadcasted_iota(jnp.int32, a.shape, 0)
    k = 1
    while k < n:
        if reverse:
            sh, valid = pltpu.roll(a, n - k, 0), row < n - k
        else:
            sh, valid = pltpu.roll(a, k, 0), row >= k
        a = jnp.maximum(a, jnp.where(valid, sh, -jnp.inf))
        k *= 2
    return a


def _chunk_order(p, nc_ctx, nch, reverse):
    if not reverse:
        return p
    return jnp.where(p < nc_ctx, nc_ctx - 1 - p, nch + nc_ctx - 1 - p)


def _mlstm_kernel(q_ref, k_ref, v_ref, g_ref, bias_ref, ei_ref, ef_ref, h_ref, c_sc, n_sc, m_sc, *, reverse):
    @pl.when(pl.program_id(0) == 0)
    def _():
        c_sc[...] = jnp.zeros_like(c_sc)
        n_sc[...] = jnp.zeros_like(n_sc)
        m_sc[...] = jnp.zeros_like(m_sc)

    eye, causal, _, blk, tri = _chunk_masks(reverse)
    last = 0 if reverse else CH - 1
    R = range(q_ref.shape[0])
    q4 = [q_ref[b] for b in R]
    k4 = [k_ref[b] * (HD ** -0.5) for b in R]
    v4 = [v_ref[b] for b in R]
    gt = [g_ref[b] + bias_ref[...] for b in R]
    i_e = [_dot_xl(gt[b], ei_ref[0]) for b in R]
    f_e = [_dot_xl(gt[b], ef_ref[0]) for b in R]
    logf = [jnp.minimum(f_e[b], 0.0) - jnp.log(1.0 + jnp.exp(-jnp.abs(f_e[b]))) for b in R]
    b_e = [_dot_xr(tri, logf[b]) for b in R]
    m_prev = [m_sc[b, 0:1, :] for b in R]
    a_e = [i_e[b] - b_e[b] for b in R]
    rowa = [_row_form(a_e[b], eye) for b in R]
    inter = [b_e[b] + m_prev[b] for b in R]
    m_t = [jnp.maximum(inter[b], b_e[b] + _cummax(a_e[b], reverse)) for b in R]
    w_inter = [jnp.exp(inter[b] - m_t[b]) for b in R]
    dexp = [jnp.exp(jnp.where(causal, b_e[b] - m_t[b] + rowa[b], -jnp.inf)) for b in R]
    smat = [_dot(q4[b], _bd(k4[b], blk), NT) * dexp[b] for b in R]
    cbd = [c_sc[b] for b in R]
    nbd = [n_sc[b] for b in R]
    qc = [_dot(q4[b], cbd[b]) for b in R]
    qn = [_dot(q4[b], nbd[b]) for b in R]
    sv = [_dot(smat[b], _bd(v4[b], blk)) for b in R]
    ssum = [_dot_xl(smat[b], blk.astype(F32)) for b in R]
    b_last = [b_e[b][last:last + 1, :] for b in R]
    g = [b_last[b] - b_e[b] + i_e[b] for b in R]
    m_new = [jnp.maximum(b_last[b] + m_prev[b], jnp.max(g[b], axis=0, keepdims=True)) for b in R]
    ks = [k4[b] * jnp.exp(g[b] - m_new[b]) for b in R]
    cup = [_dot(ks[b], v4[b], TN) for b in R]
    nup = [_dot(ks[b], jnp.ones((CH, GW), F32), TN) for b in R]
    for b in R:
        num = w_inter[b] * qc[b] + sv[b]
        den = w_inter[b] * qn[b] + ssum[b]
        h_ref[b] = num / jnp.maximum(jnp.abs(den), jnp.exp(-m_t[b]))
        a_prev = jnp.exp(b_last[b] + m_prev[b] - m_new[b])
        c_sc[b] = a_prev * cbd[b] + jnp.where(blk, cup[b], 0.0)
        n_sc[b] = a_prev * nbd[b] + jnp.where(blk, nup[b], 0.0)
        m_sc[b] = jnp.broadcast_to(m_new[b], (8, GW))


def _mlstm_call(U, bias128, e_i, e_f, B, T, CT, reverse):
    nch, ncc = T // CH, CT // CH
    U3 = U.reshape(B, T, U.shape[1])
    cmap = lambda col: (lambda p: (0, _chunk_order(p, ncc, nch, reverse), col))
    d = 1 if reverse else 0
    out = pl.pallas_call(
        functools.partial(_mlstm_kernel, reverse=reverse), grid=(nch,),
        in_specs=[pl.BlockSpec((B, CH, GW), cmap(U_CQ // GW)),
                  pl.BlockSpec((B, CH, GW), cmap(U_CK // GW)),
                  pl.BlockSpec((B, CH, GW), cmap(U_CV // GW)),
                  pl.BlockSpec((B, CH, LANES), cmap(U_G // LANES)),
                  pl.BlockSpec((1, LANES), lambda p: (0, 0)),
                  pl.BlockSpec((1, LANES, GW), lambda p: (d, 0, 0)),
                  pl.BlockSpec((1, LANES, GW), lambda p: (d, 0, 0))],
        out_specs=pl.BlockSpec((B, CH, GW), cmap(0)),
        out_shape=jax.ShapeDtypeStruct((B, T, GW), F32),
        scratch_shapes=[pltpu.VMEM((B, GW, GW), F32), pltpu.VMEM((B, GW, GW), F32), pltpu.VMEM((B, 8, GW), F32)],
        compiler_params=_cparams(("arbitrary",)),
        name="mlstm_bwd" if reverse else "mlstm_fwd",
    )(U3, U3, U3, U3, bias128, e_i, e_f)
    return out.reshape(B * T, GW)


def _gdn_prep_kernel(prev_ref, cur_ref, next_ref, g_ref, conv_ref, g64_ref, e_ref, alog_ref, dtb_ref,
                     q_ref, k_ref, v_ref, bf_ref, gf_ref, bb_ref, gb_ref, *, n_ctx_tiles, nt):
    i = pl.program_id(1)
    at_start = (i == 0) | (i == n_ctx_tiles)
    at_end = (i == n_ctx_tiles - 1) | (i == nt - 1)
    cur = cur_ref[...]
    head = jnp.where(at_start, 0.0, prev_ref[TM - 8:TM, :])
    tail = jnp.where(at_end, 0.0, next_ref[0:8, :])
    xx = jnp.concatenate([head, cur, tail], axis=0)
    cw = conv_ref[...]
    kw = cw.shape[0]
    acc = jnp.zeros_like(cur)
    for j in range(kw):
        s0 = 8 + j - kw // 2
        acc = acc + xx[s0:s0 + TM, :] * cw[j:j + 1, :]
    t = _silu(acc)
    q, k, v = t[:, 0:GW], t[:, GW:2 * GW], t[:, 2 * GW:3 * GW]
    g64 = g64_ref[...] * float(HD)
    q_ref[...] = q * lax.rsqrt(_dot_xl(q * q, g64) + EPS) * (HD ** -0.5)
    k_ref[...] = k * lax.rsqrt(_dot_xl(k * k, g64) + EPS)
    v_ref[...] = v
    gt = g_ref[...]
    for d, (b_out, g_out) in enumerate(((bf_ref, gf_ref), (bb_ref, gb_ref))):
        b_out[...] = _sigmoid(_dot_xl(gt, e_ref[2 * d]))
        a_e = _dot_xl(gt, e_ref[2 * d + 1])
        g_out[...] = -jnp.exp(alog_ref[d:d + 1, :]) * _softplus(a_e + dtb_ref[d:d + 1, :])


def _gdn_prep_call(U, conv_w, g64, e_d, alog_e, dtb_e, B, T, CT):
    nt = T // TM
    w3 = 3 * GW
    row = lambda b, i: (b * nt + i, 0)
    const2 = lambda b, i: (0, 0)
    o = jax.ShapeDtypeStruct((B * T, GW), F32)
    return pl.pallas_call(
        functools.partial(_gdn_prep_kernel, n_ctx_tiles=CT // TM, nt=nt), grid=(B, nt),
        in_specs=[pl.BlockSpec((TM, w3), lambda b, i: (b * nt + jnp.maximum(i - 1, 0), 0)),
                  pl.BlockSpec((TM, w3), row),
                  pl.BlockSpec((TM, w3), lambda b, i: (b * nt + jnp.minimum(i + 1, nt - 1), 0)),
                  pl.BlockSpec((TM, LANES), lambda b, i: (b * nt + i, U_G // LANES)),
                  pl.BlockSpec(conv_w.shape, const2),
                  pl.BlockSpec((GW, GW), const2),
                  pl.BlockSpec(e_d.shape, lambda b, i: (0, 0, 0)),
                  pl.BlockSpec((2, GW), const2), pl.BlockSpec((2, GW), const2)],
        out_specs=[pl.BlockSpec((TM, GW), row)] * 7,
        out_shape=[o] * 7,
        compiler_params=_cparams(("arbitrary", "arbitrary")), name="gdn_prep",
    )(U, U, U, U, conv_w, g64, e_d, alog_e, dtb_e)


def _gdn_kernel(q_ref, k_ref, v_ref, beta_ref, g_ref, o_ref, s_sc, *, reverse):
    @pl.when(pl.program_id(0) == 0)
    def _():
        s_sc[...] = jnp.zeros_like(s_sc)

    eye, causal, strict, blk, tri = _chunk_masks(reverse)
    last = 0 if reverse else CH - 1
    R = range(q_ref.shape[0])
    q4, k4, v4, beta = ([r[b] for b in R] for r in (q_ref, k_ref, v_ref, beta_ref))
    gam = [_dot_xr(tri, g_ref[b]) for b in R]
    rowg = [_row_form(gam[b], eye) for b in R]
    decay = [jnp.exp(jnp.where(causal, gam[b] - rowg[b], -jnp.inf)) for b in R]
    kbeta = [k4[b] * beta[b] for b in R]
    kbd = [_bd(k4[b], blk) for b in R]
    a4 = [jnp.where(strict, _dot(kbeta[b], kbd[b], NT) * decay[b], 0.0) for b in R]
    attn = [_dot(q4[b], kbd[b], NT) * decay[b] for b in R]
    tinv = [jnp.where(eye, 1.0, 0.0) - a4[b] for b in R]
    apow = a4
    for _ in range(5):
        apow = [_dot_x3(apow[b], _bd(apow[b], blk)) for b in R]
        tinv = [tinv[b] + _dot_x3(tinv[b], _bd(apow[b], blk)) for b in R]
    u4 = [_dot_x3(tinv[b], _bd(v4[b] * beta[b], blk)) for b in R]
    w4 = [_dot_x3(tinv[b], _bd(kbeta[b] * jnp.exp(gam[b]), blk)) for b in R]
    sbd = [s_sc[b] for b in R]
    v_new = [u4[b] - _dot(w4[b], sbd[b]) for b in R]
    o_inter = [_dot(q4[b] * jnp.exp(gam[b]), sbd[b]) for b in R]
    o_intra = [_dot(attn[b], _bd(v_new[b], blk)) for b in R]
    upd = [_dot(k4[b] * jnp.exp(gam[b][last:last + 1, :] - gam[b]), v_new[b], TN) for b in R]
    for b in R:
        o_ref[b] = o_inter[b] + o_intra[b]
        s_sc[b] = sbd[b] * jnp.exp(gam[b][last:last + 1, :]) + jnp.where(blk, upd[b], 0.0)


def _gdn_call(q, k, v, beta, g, B, T, CT, reverse):
    nch, ncc = T // CH, CT // CH
    spec = pl.BlockSpec((B, CH, GW), lambda p: (0, _chunk_order(p, ncc, nch, reverse), 0))
    r3 = lambda a: a.reshape(B, T, GW)
    out = pl.pallas_call(
        functools.partial(_gdn_kernel, reverse=reverse), grid=(nch,),
        in_specs=[spec] * 5, out_specs=spec,
        out_shape=jax.ShapeDtypeStruct((B, T, GW), F32),
        scratch_shapes=[pltpu.VMEM((B, GW, GW), F32)],
        compiler_params=_cparams(("arbitrary",)),
        name="gdn_bwd" if reverse else "gdn_fwd",
    )(r3(q), r3(k), r3(v), r3(beta), r3(g))
    return out.reshape(B * T, GW)


def _post_kernel(x_ref, mod_ref, oa_ref, ob_ref, hcf_ref, hcb_ref, odf_ref, odb_ref, oc_ref, zd_ref,
                 gains_ref, g64_ref, w_ref, xo_ref, *, lam_init):
    gains = gains_ref[...]
    g64 = g64_ref

    oa = oa_ref[...]
    ya = oa * lax.rsqrt(_group_mean(oa * oa, g64) + EPS) * gains[0:1] * (1.0 - lam_init)
    yb = ob_ref[...]
    hc = hcf_ref[...] + hcb_ref[...]
    dc = hc - _group_mean(hc, g64)
    yc = _sigmoid(oc_ref[...]) * (dc * lax.rsqrt(_group_mean(dc * dc, g64) + EPS) * gains[1:2])
    od = odf_ref[...] + odb_ref[...]
    yd = od * lax.rsqrt(_group_mean(od * od, g64) + EPS) * gains[2:3] * _silu(zd_ref[...])
    out = _dot(ya, w_ref[0:GW, :])
    out = out + _dot(yb, w_ref[GW:2 * GW, :])
    out = out + _dot(yc, w_ref[2 * GW:3 * GW, :])
    out = out + _dot(yd, w_ref[3 * GW:4 * GW, :])
    xo_ref[...] = x_ref[...] + mod_ref[0, 0][2:3] * out


def _post_call(X, modtab, oa, ob, hcf, hcb, odf, odb, U, gains, g64, w_out, B, T, lam_init):
    nt = T // TM
    d = X.shape[1]
    row = lambda b, i: (b * nt + i, 0)
    const = lambda b, i: (0, 0)
    gspec = pl.BlockSpec((TM, GW), row)
    return pl.pallas_call(
        functools.partial(_post_kernel, lam_init=lam_init), grid=(B, nt),
        in_specs=[pl.BlockSpec((TM, d), row),
                  pl.BlockSpec((1, 1, 8, d), lambda b, i: (b, jnp.minimum(i, 1), 0, 0)),
                  gspec, gspec, gspec, gspec, gspec, gspec,
                  pl.BlockSpec((TM, GW), lambda b, i: (b * nt + i, U_CO // GW)),
                  pl.BlockSpec((TM, GW), lambda b, i: (b * nt + i, U_DZ // GW)),
                  pl.BlockSpec((8, GW), const), pl.BlockSpec((GW, GW), const),
                  pl.BlockSpec(w_out.shape, const)],
        out_specs=pl.BlockSpec((TM, d), row),
        out_shape=jax.ShapeDtypeStruct(X.shape, F32),
        compiler_params=_cparams(("arbitrary", "arbitrary")), name="out_proj",
    )(X, modtab, oa, ob, hcf, hcb, odf, odb, U, U, gains, g64, w_out)


FF_CHUNK = 256


def _ffn_kernel(x_ref, mod_ref, gain_ref, wg_ref, wu_ref, wd_ref, xo_ref):
    x = x_ref[...]
    mod = mod_ref[0, 0]
    h = _norm_mod(x, gain_ref[...], mod[3:4], mod[4:5]).astype(BF16)
    acc = jnp.zeros(x.shape, F32)
    for f in range(0, wg_ref.shape[1], FF_CHUNK):
        g = lax.dot_general(h, wg_ref[:, f:f + FF_CHUNK], NN, preferred_element_type=F32)
        u = lax.dot_general(h, wu_ref[:, f:f + FF_CHUNK], NN, preferred_element_type=F32)
        acc = acc + _dot(_silu(g) * u, wd_ref[f:f + FF_CHUNK, :])
    xo_ref[...] = x + mod[5:6] * acc


def _ffn_call(X, modtab, gain, wg, wu, wd, nb, tiles_per_b, kind_of_tile):
    d = X.shape[1]
    row = lambda b, i: (b * tiles_per_b + i, 0)
    const = lambda b, i: (0, 0)
    return pl.pallas_call(
        _ffn_kernel, grid=(nb, tiles_per_b),
        in_specs=[pl.BlockSpec((TM, d), row),
                  pl.BlockSpec((1, 1, 8, d), lambda b, i: (b, kind_of_tile(i), 0, 0)),
                  pl.BlockSpec((1, d), const),
                  pl.BlockSpec(wg.shape, const), pl.BlockSpec(wu.shape, const), pl.BlockSpec(wd.shape, const)],
        out_specs=pl.BlockSpec((TM, d), row),
        out_shape=jax.ShapeDtypeStruct(X.shape, F32),
        compiler_params=_cparams(("arbitrary", "arbitrary")), name="ffn_dense",
    )(X, modtab, gain, wg, wu, wd)


def _router_kernel(x_ref, mod_ref, gain_ref, r_ref, h_ref, gates_ref, *, n_exp):
    x = x_ref[...]
    mod = mod_ref[0, 0]
    h = _norm_mod(x, gain_ref[...], mod[3:4], mod[4:5])
    h_ref[...] = h.astype(BF16)
    logits = lax.dot_general(h, r_ref[...], NN, precision=lax.Precision.HIGHEST, preferred_element_type=F32)
    lane = lax.broadcasted_iota(jnp.int32, logits.shape, 1)
    neg = -jnp.inf
    lg = jnp.where(lane < n_exp, logits, neg)
    v1 = jnp.max(lg, axis=1, keepdims=True)
    i1 = jnp.min(jnp.where(lg == v1, lane, LANES), axis=1, keepdims=True)
    lg2 = jnp.where(lane == i1, neg, lg)
    v2 = jnp.max(lg2, axis=1, keepdims=True)
    i2 = jnp.min(jnp.where(lg2 == v2, lane, LANES), axis=1, keepdims=True)
    e2 = jnp.exp(v2 - v1)
    w1 = 1.0 / (1.0 + e2)
    w2 = e2 / (1.0 + e2)
    gates_ref[...] = jnp.where(lane == i1, w1, 0.0) + jnp.where(lane == i2, w2, 0.0)


def _router_call(X, modtab, gain, router128, nb, tiles_per_b, n_exp):
    d = X.shape[1]
    row = lambda b, i: (b * tiles_per_b + i, 0)
    const = lambda b, i: (0, 0)
    return pl.pallas_call(
        functools.partial(_router_kernel, n_exp=n_exp), grid=(nb, tiles_per_b),
        in_specs=[pl.BlockSpec((TM, d), row),
                  pl.BlockSpec((1, 1, 8, d), lambda b, i: (b, 1, 0, 0)),
                  pl.BlockSpec((1, d), const),
                  pl.BlockSpec(router128.shape, const)],
        out_specs=[pl.BlockSpec((TM, d), row), pl.BlockSpec((TM, LANES), row)],
        out_shape=[jax.ShapeDtypeStruct(X.shape, BF16), jax.ShapeDtypeStruct((X.shape[0], LANES), F32)],
        compiler_params=_cparams(("arbitrary", "arbitrary")), name="moe_router",
    )(X, modtab, gain, router128)


MOE_TM = 1024
MOE_CH = 256
MOE_TF = 512


def _moe_kernel(x_ref, mod_ref, h_ref, gates_ref, tri_ref, wg_ref, wu_ref, wd_ref, xo_ref,
                acc_sc, rk_sc, rkt_sc, xg_sc, y_sc, nck_sm):
    e = pl.program_id(2)
    f = pl.program_id(3)
    ne = pl.num_programs(2)
    nf = pl.num_programs(3)
    tm = h_ref.shape[0]

    @pl.when((e == 0) & (f == 0))
    def _():
        acc_sc[...] = jnp.zeros_like(acc_sc)
        sel = gates_ref[...] > 0.0
        rank = lax.dot_general(tri_ref[...], jnp.where(sel, 1.0, 0.0).astype(BF16), NN,
                               preferred_element_type=F32)
        rk = jnp.where(sel, rank, -1.0)
        rk_sc[...] = rk
        rkt_sc[...] = rk.T

    @pl.when(f == 0)
    def _():
        rrow = rkt_sc[pl.ds(e, 1), :]
        cnt = jnp.sum(jnp.where(rrow >= 0.0, 1.0, 0.0))
        nck_sm[0] = ((cnt + (MOE_CH - 1.0)) * (1.0 / MOE_CH)).astype(jnp.int32)
        h = h_ref[...]

        def gather(k, carry):
            r = lax.broadcasted_iota(jnp.int32, (MOE_CH, 1), 0).astype(F32) + (k * MOE_CH).astype(F32)
            onehot = jnp.where(r == rrow, 1.0, 0.0).astype(BF16)
            xg_sc[k] = lax.dot_general(onehot, h, NN, preferred_element_type=F32).astype(BF16)
            y_sc[k] = jnp.zeros((MOE_CH, y_sc.shape[2]), F32)
            return carry

        lax.fori_loop(0, nck_sm[0], gather, 0)

    def expert(k, carry):
        xk = xg_sc[k]
        g = lax.dot_general(xk, wg_ref[0], NN, preferred_element_type=F32)
        u = lax.dot_general(xk, wu_ref[0], NN, preferred_element_type=F32)
        y_sc[k] += _dot(_silu(g) * u, wd_ref[0])
        return carry

    lax.fori_loop(0, nck_sm[0], expert, 0)

    @pl.when(f == nf - 1)
    def _():
        lane = lax.broadcasted_iota(jnp.int32, (tm, LANES), 1)
        rcol = jnp.sum(jnp.where(lane == e, rk_sc[...], 0.0), axis=1, keepdims=True)
        gcol = jnp.sum(jnp.where(lane == e, gates_ref[...], 0.0), axis=1, keepdims=True)

        def scatter(k, carry):
            c = lax.broadcasted_iota(jnp.int32, (1, MOE_CH), 1).astype(F32) + (k * MOE_CH).astype(F32)
            onehot_t = jnp.where(rcol == c, 1.0, 0.0).astype(BF16)
            acc_sc[...] += gcol * lax.dot_general(onehot_t, y_sc[k].astype(BF16), NN, preferred_element_type=F32)
            return carry

        lax.fori_loop(0, nck_sm[0], scatter, 0)

    @pl.when((e == ne - 1) & (f == nf - 1))
    def _():
        xo_ref[...] = x_ref[...] + mod_ref[0, 0][5:6] * acc_sc[...]


def _moe_call(X, modtab, hb, gates, wg, wu, wd, nb, rows_per_b):
    d = X.shape[1]
    n_exp, _, ff = wg.shape
    tm = min(MOE_TM, rows_per_b)
    tpb = rows_per_b // tm
    nck = tm // MOE_CH
    tri = jnp.asarray(np.tril(np.ones((tm, tm), np.float32), -1), BF16)
    row = lambda b, i, e, f: (b * tpb + i, 0)
    return pl.pallas_call(
        _moe_kernel, grid=(nb, tpb, n_exp, ff // MOE_TF),
        in_specs=[pl.BlockSpec((tm, d), row),
                  pl.BlockSpec((1, 1, 8, d), lambda b, i, e, f: (b, 1, 0, 0)),
                  pl.BlockSpec((tm, d), row),
                  pl.BlockSpec((tm, LANES), row),
                  pl.BlockSpec((tm, tm), lambda b, i, e, f: (0, 0)),
                  pl.BlockSpec((1, d, MOE_TF), lambda b, i, e, f: (e, 0, f)),
                  pl.BlockSpec((1, d, MOE_TF), lambda b, i, e, f: (e, 0, f)),
                  pl.BlockSpec((1, MOE_TF, d), lambda b, i, e, f: (e, f, 0))],
        out_specs=pl.BlockSpec((tm, d), row),
        out_shape=jax.ShapeDtypeStruct(X.shape, F32),
        scratch_shapes=[pltpu.VMEM((tm, d), F32), pltpu.VMEM((tm, LANES), F32), pltpu.VMEM((LANES, tm), F32),
                        pltpu.VMEM((nck, MOE_CH, d), BF16), pltpu.VMEM((nck, MOE_CH, d), F32),
                        pltpu.SMEM((1,), jnp.int32)],
        compiler_params=_cparams(("arbitrary",) * 4), name="moe_experts",
    )(X, modtab, hb, gates, tri, wg, wu, wd)


def _group_mean_matrix(group):
    idx = np.arange(GW)
    return jnp.asarray((idx[:, None] // group == idx[None, :] // group).astype(np.float32) / group, BF16)


def _rope_tables(n_lat, n_ctx, dim, reps):
    t = jnp.arange(n_lat)
    pos = jnp.stack([(t // GRID_W).astype(F32), (t % GRID_W).astype(F32)], axis=1)
    n_freq = dim // 4
    inv_freq = ROPE_THETA ** (-jnp.arange(n_freq, dtype=F32) / n_freq)
    ang = pos[:, :, None] * inv_freq
    cos = jnp.cos(ang)
    sin = jnp.sin(ang)
    cos = jnp.stack([cos, cos], axis=2).reshape(n_lat, dim)
    sin = jnp.stack([-sin, sin], axis=2).reshape(n_lat, dim)
    cos = jnp.concatenate([jnp.ones((n_ctx, dim), F32), cos], axis=0)
    sin = jnp.concatenate([jnp.zeros((n_ctx, dim), F32), sin], axis=0)
    return jnp.tile(cos, (1, reps)), jnp.tile(sin, (1, reps))


def _expand_matrix(col0):
    m = np.zeros((LANES, GW), np.float32)
    for h in range(4):
        m[col0 + h, h * HD:(h + 1) * HD] = 1.0
    return m


_SPLITS = (256, 256, 256, 256, 128, 128, 256, 256, 256, 256, 16, 768, 256, 16)


def _reorder_w_in(w):
    offs = np.concatenate([[0], np.cumsum(_SPLITS)])
    seg = [w[:, offs[i]:offs[i + 1]] for i in range(len(_SPLITS))]
    aq, ak, av, bq, bk, bv, cq, ck, cv, co, cg, dqkv, dz, dg = seg
    pad = jnp.zeros((w.shape[0], LANES - 32), w.dtype)
    return jnp.concatenate([aq, ak, av, bq, bk, bv, dqkv, cq, ck, cv, co, dz, cg, dg, pad], axis=1).astype(BF16)


def kernel(x, c, ctx, c_ctx, w_mod, b_mod, norm1, norm2, w_in, w_out, diff_qk_gain, diff_lambda, diff_subln,
           gqa_qk_gain, mlstm_gate_bias, mlstm_norm, gdn_conv, gdn_a_log, gdn_dt_bias, gdn_norm,
           ffn_w_gate, ffn_w_up, ffn_w_down, moe_router, moe_w_gate, moe_w_up, moe_w_down):
    B, S, D = x.shape
    CT = ctx.shape[1]
    T = CT + S
    depth = w_in.shape[0]
    assert CT % TM == 0 and S % TM == 0 and S % GRID_W == 0 and B + 1 <= 8

    g32 = _group_mean_matrix(HD // 2)
    g64 = _group_mean_matrix(HD)
    cosa, sina = _rope_tables(S, CT, HD // 2, 8)
    cosb, sinb = _rope_tables(S, CT, HD, 4)
    e_i = jnp.asarray(np.stack([_expand_matrix(0), _expand_matrix(8)]), F32)
    e_f = jnp.asarray(np.stack([_expand_matrix(4), _expand_matrix(12)]), F32)
    e_d = jnp.asarray(np.stack([_expand_matrix(16 + 4 * j) for j in range(4)]), F32)

    cvec = jnp.zeros((8, D), F32).at[:B].set(c).at[B].set(c_ctx)
    mod_all = _mod_call(cvec, w_mod, b_mod).reshape(depth, 8, 6, D)

    X = jnp.concatenate([ctx, x], axis=1).reshape(B * T, D)
    nb_rows = T
    for li in range(depth):
        with_ctx = li < depth - 1
        lam_init = 0.8 - 0.6 * math.exp(-0.3 * li)
        m = mod_all[li]
        modtab = jnp.stack([jnp.broadcast_to(m[B], (B, 6, D)), m[:B]], axis=1)
        modtab = jnp.pad(modtab, ((0, 0), (0, 0), (0, 2), (0, 0)))

        qkg = jnp.zeros((8, GW), F32)
        qkg = qkg.at[0].set(jnp.tile(diff_qk_gain[li, 0], 8)).at[1].set(jnp.tile(diff_qk_gain[li, 1], 8))
        qkg = qkg.at[2].set(jnp.tile(gqa_qk_gain[li, 0], 4)).at[3].set(jnp.tile(gqa_qk_gain[li, 1], 4))
        qta, ka, vta, qtb, kb, vtb, U = _kin_call(
            X, modtab, norm1[li][None, :], _reorder_w_in(w_in[li]), g32, g64, qkg, cosa, sina, cosb, sinb, B, T)

        oa = _attn_call(qta, ka, vta, diff_lambda[li], B, T, CT, True, lam_init, "attn_diff")
        ob = _attn_call(qtb, kb, vtb, diff_lambda[li], B, T, CT, False, lam_init, "attn_gqa")

        bias128 = jnp.zeros((1, LANES), F32).at[0, :16].set(mlstm_gate_bias[li].reshape(-1))
        hcf = _mlstm_call(U, bias128, e_i, e_f, B, T, CT, False)
        hcb = _mlstm_call(U, bias128, e_i, e_f, B, T, CT, True)

        alog_e = jnp.repeat(gdn_a_log[li], HD, axis=1)
        dtb_e = jnp.repeat(gdn_dt_bias[li], HD, axis=1)
        qd, kd, vd, bf, gf, bb, gb = _gdn_prep_call(U, gdn_conv[li], g64, e_d, alog_e, dtb_e, B, T, CT)
        odf = _gdn_call(qd, kd, vd, bf, gf, B, T, CT, False)
        odb = _gdn_call(qd, kd, vd, bb, gb, B, T, CT, True)

        gains = jnp.zeros((8, GW), F32)
        gains = gains.at[0].set(jnp.tile(diff_subln[li], 4)).at[1].set(mlstm_norm[li])
        gains = gains.at[2].set(jnp.tile(gdn_norm[li], 4))
        X = _post_call(X, modtab, oa, ob, hcf, hcb, odf, odb, U, gains, g64, w_out[li].astype(BF16), B, T, lam_init)

        if with_ctx:
            tiles_per_b, kind = T // TM, (lambda i: jnp.minimum(i, 1))
        else:
            X = X.reshape(B, T, D)[:, CT:].reshape(B * S, D)
            nb_rows = S
            tiles_per_b, kind = S // TM, (lambda i: 1)
        j = li // 2
        if li % 2 == 0:
            X = _ffn_call(X, modtab, norm2[li][None, :], ffn_w_gate[j].astype(BF16), ffn_w_up[j].astype(BF16),
                          ffn_w_down[j].astype(BF16), B, tiles_per_b, kind)
        else:
            assert not with_ctx
            n_exp = moe_router.shape[2]
            router128 = jnp.pad(moe_router[j], ((0, 0), (0, LANES - n_exp)))
            hb, gates = _router_call(X, modtab, norm2[li][None, :], router128, B, tiles_per_b, n_exp)
            X = _moe_call(X, modtab, hb, gates, moe_w_gate[j].astype(BF16), moe_w_up[j].astype(BF16),
                          moe_w_down[j].astype(BF16), B, nb_rows)
    return X.reshape(B, nb_rows, D)[:, nb_rows - S:]
```

```python
import functools
import math

import numpy as np
import jax
import jax.numpy as jnp
from jax import lax
from jax.experimental import pallas as pl
from jax.experimental.pallas import tpu as pltpu

F32 = jnp.float32
BF16 = jnp.bfloat16
EPS = 1e-6
ROPE_THETA = 10000.0
GRID_W = 64
LOG2E = 1.4426950408889634

TM = 256
CH = 64
HD = 64
GW = 256
LANES = 128
KV_BLKS = 2
V_ONES = 16
VMEM_LIMIT = 56 * 1024 * 1024

NN = (((1,), (0,)), ((), ()))
NT = (((1,), (1,)), ((), ()))
TN = (((0,), (0,)), ((), ()))


def _dot(a, b, dims=NN):
    return lax.dot_general(a.astype(BF16), b.astype(BF16), dims, preferred_element_type=F32)


def _split(a):
    hi = a.astype(BF16)
    lo = (a - hi.astype(F32)).astype(BF16)
    return hi, lo


def _dot_xl(a, b_exact, dims=NN):
    hi, lo = _split(a)
    bb = b_exact.astype(BF16)
    return (lax.dot_general(hi, bb, dims, preferred_element_type=F32)
            + lax.dot_general(lo, bb, dims, preferred_element_type=F32))


def _dot_xr(a_exact, b, dims=NN):
    hi, lo = _split(b)
    aa = a_exact.astype(BF16)
    return (lax.dot_general(aa, hi, dims, preferred_element_type=F32)
            + lax.dot_general(aa, lo, dims, preferred_element_type=F32))


def _dot_x3(a, b, dims=NN):
    ah, al = _split(a)
    bh, bl = _split(b)
    d = functools.partial(lax.dot_general, dimension_numbers=dims, preferred_element_type=F32)
    return d(ah, bh) + d(ah, bl) + d(al, bh)


def _sigmoid(x):
    return 1.0 / (1.0 + jnp.exp(-x))


def _silu(x):
    return x * _sigmoid(x)


def _softplus(x):
    return jnp.maximum(x, 0.0) + jnp.log(1.0 + jnp.exp(-jnp.abs(x)))


def _cparams(sem, vmem=VMEM_LIMIT):
    return pltpu.CompilerParams(dimension_semantics=sem, vmem_limit_bytes=vmem)


def _mod_kernel(c_ref, w_ref, b_ref, o_ref):
    cv = c_ref[...]
    s = _silu(cv)
    o_ref[0] = lax.dot_general(s, w_ref[0], NN, precision=lax.Precision.HIGHEST,
                               preferred_element_type=F32) + b_ref[0]


def _mod_call(cvec, w_mod, b_mod):
    depth, d, n6 = w_mod.shape
    rows = cvec.shape[0]
    tn = 1536
    return pl.pallas_call(
        _mod_kernel,
        grid=(depth, n6 // tn),
        in_specs=[pl.BlockSpec((rows, d), lambda l, j: (0, 0)),
                  pl.BlockSpec((1, d, tn), lambda l, j: (l, 0, j)),
                  pl.BlockSpec((1, 1, tn), lambda l, j: (l, 0, j))],
        out_specs=pl.BlockSpec((1, rows, tn), lambda l, j: (l, 0, j)),
        out_shape=jax.ShapeDtypeStruct((depth, rows, n6), F32),
        compiler_params=_cparams(("arbitrary", "arbitrary")),
        name="adaln_mod",
    )(cvec, w_mod, b_mod.reshape(depth, 1, n6))


def _norm_mod(x, gain, shift, scale):
    ms = jnp.mean(x * x, axis=-1, keepdims=True)
    return (x * lax.rsqrt(ms + EPS) * gain) * (1.0 + scale) + shift


def _group_mean(v, g_ref):
    return _dot_xl(v, g_ref[...])


def _rope(xn, cos, sin, half):
    w = xn.shape[1]
    lane = lax.broadcasted_iota(jnp.int32, xn.shape, 1)
    first = (lane % (2 * half)) < half
    xs = jnp.where(first, pltpu.roll(xn, w - half, 1), pltpu.roll(xn, half, 1))
    return xn * cos + xs * sin


A_W = 3 * GW
B_W = GW + 2 * 128
AB_W = A_W + B_W
U_W = 2176
U_DQKV, U_CQ, U_CK, U_CV, U_CO, U_DZ, U_G = 0, 768, 1024, 1280, 1536, 1792, 2048


def _kin_kernel(x_ref, mod_ref, gain_ref, w_ref, g32_ref, g64_ref, qkg_ref,
                cosa_ref, sina_ref, cosb_ref, sinb_ref,
                qta_ref, ka_ref, vta_ref, qtb_ref, kb_ref, vtb_ref, u_ref):
    x = x_ref[...]
    mod = mod_ref[0, 0]
    h = _norm_mod(x, gain_ref[...], mod[0:1], mod[1:2]).astype(BF16)

    ya = lax.dot_general(h, w_ref[:, 0:A_W], NN, preferred_element_type=F32)
    qkg = qkg_ref[...]
    ca, sa = cosa_ref[...], sina_ref[...]

    def prep(v, g_ref, gain, cos, sin, half):
        vn = v * lax.rsqrt(_group_mean(v * v, g_ref) + EPS) * gain
        return _rope(vn, cos, sin, half)

    qa = prep(ya[:, 0:GW], g32_ref, qkg[0:1], ca, sa, 8) * ((HD // 2) ** -0.5 * LOG2E)
    ka = prep(ya[:, GW:2 * GW], g32_ref, qkg[1:2], ca, sa, 8)
    qta_ref[0, 0] = qa.T.astype(BF16)
    ka_ref[...] = ka.astype(BF16)
    vta_ref[0, 0] = ya[:, 2 * GW:3 * GW].T.astype(BF16)

    yb = lax.dot_general(h, w_ref[:, A_W:AB_W], NN, preferred_element_type=F32)
    cb, sb = cosb_ref[...], sinb_ref[...]
    qb = prep(yb[:, 0:GW], g64_ref, qkg[2:3], cb, sb, 16) * (HD ** -0.5 * LOG2E)
    kb = prep(yb[:, GW:GW + 128], g64_ref.at[0:128, 0:128], qkg[3:4, 0:128], cb[:, 0:128], sb[:, 0:128], 16)
    qtb_ref[0, 0] = qb.T.astype(BF16)
    kb_ref[...] = kb.astype(BF16)
    vtb_ref[0, 0] = yb[:, GW + 128:GW + 256].T.astype(BF16)

    u_ref[...] = lax.dot_general(h, w_ref[:, AB_W:AB_W + U_W], NN, preferred_element_type=F32)


def _kin_call(X, modtab, gain, w_r, g32, g64, qkg, cosa, sina, cosb, sinb, B, T):
    nt = T // TM
    d = X.shape[1]
    wcols = w_r.shape[1]
    row = lambda b, i: (b * nt + i, 0)
    const = lambda b, i: (0, 0)
    pos = lambda b, i: (i, 0)
    tmap = lambda b, i: (b, i, 0, 0)
    in_specs = [
        pl.BlockSpec((TM, d), row),
        pl.BlockSpec((1, 1, 8, d), lambda b, i: (b, jnp.minimum(i, 1), 0, 0)),
        pl.BlockSpec((1, d), const),
        pl.BlockSpec((d, wcols), const),
        pl.BlockSpec((GW, GW), const),
        pl.BlockSpec((GW, GW), const),
        pl.BlockSpec((8, GW), const),
        pl.BlockSpec((TM, GW), pos), pl.BlockSpec((TM, GW), pos),
        pl.BlockSpec((TM, GW), pos), pl.BlockSpec((TM, GW), pos),
    ]
    out_specs = [
        pl.BlockSpec((1, 1, GW, TM), tmap), pl.BlockSpec((TM, GW), row), pl.BlockSpec((1, 1, GW, TM), tmap),
        pl.BlockSpec((1, 1, GW, TM), tmap), pl.BlockSpec((TM, 128), row), pl.BlockSpec((1, 1, 128, TM), tmap),
        pl.BlockSpec((TM, U_W), row),
    ]
    out_shape = [
        jax.ShapeDtypeStruct((B, nt, GW, TM), BF16), jax.ShapeDtypeStruct((B * T, GW), BF16),
        jax.ShapeDtypeStruct((B, nt, GW, TM), BF16),
        jax.ShapeDtypeStruct((B, nt, GW, TM), BF16), jax.ShapeDtypeStruct((B * T, 128), BF16),
        jax.ShapeDtypeStruct((B, nt, 128, TM), BF16),
        jax.ShapeDtypeStruct((B * T, U_W), F32),
    ]
    return pl.pallas_call(
        _kin_kernel, grid=(B, nt), in_specs=in_specs, out_specs=out_specs, out_shape=out_shape,
        compiler_params=_cparams(("arbitrary", "arbitrary")), name="in_proj",
    )(X, modtab, gain, w_r, g32, g64, qkg, cosa, sina, cosb, sinb)


def _attn_kernel(qt_ref, k_ref, vt_ref, lam_ref, o_ref, sa_sc, sb_sc, *,
                 diff, lam_init, n_ctx_tiles, n_ctx_blk, n_all_blk):
    i = pl.program_id(1)
    n_lat_pairs = jnp.where(i < n_ctx_tiles, 0, (n_all_blk - n_ctx_blk) // (2 * KV_BLKS))
    qt = qt_ref[0, 0]
    kd = k_ref.shape[1]
    rowi = lax.broadcasted_iota(jnp.int32, (kd, TM), 0)

    def run_streams(qms, vlos):
        def scores(j0, nblk):
            off = pl.multiple_of(j0 * TM, TM)
            kb = k_ref[pl.ds(off, nblk * TM), :]
            return [lax.dot_general(kb, qm, NN, preferred_element_type=F32) for qm in qms]

        def update(ss, j0, carry, nblk):
            stats = []
            for (m, _), s in zip(carry, ss):
                m_new = jnp.maximum(m, jnp.max(s, axis=0, keepdims=True))
                stats.append((m_new, jnp.exp2(m - m_new), jnp.exp2((s - m_new).astype(BF16))))
            ones = jnp.ones((V_ONES, nblk * TM), BF16)
            new = []
            for (_, acc), (m_new, alpha, p), vlo in zip(carry, stats, vlos):
                vb = jnp.concatenate([vt_ref[0, j0 + t, vlo:vlo + HD, :] for t in range(nblk)], axis=1)
                acc = alpha * acc + lax.dot_general(jnp.concatenate([vb, ones], axis=0), p, NN,
                                                    preferred_element_type=F32)
                new.append((m_new, acc))
            return tuple(new)

        def scores_into(buf, j0):
            for si, s in enumerate(scores(j0, KV_BLKS)):
                buf[si] = s

        def pair(it, carry):
            ja = n_ctx_blk + it * (2 * KV_BLKS)
            jb = ja + KV_BLKS
            jn = jnp.minimum(jb + KV_BLKS, n_all_blk - KV_BLKS)
            scores_into(sb_sc, jb)
            carry = update([sa_sc[si] for si in range(len(qms))], ja, carry, KV_BLKS)
            scores_into(sa_sc, jn)
            return update([sb_sc[si] for si in range(len(qms))], jb, carry, KV_BLKS)

        m0 = jnp.full((1, TM), -jnp.inf, F32)
        a0 = jnp.zeros((HD + V_ONES, TM), F32)
        scores_into(sa_sc, n_ctx_blk)
        res = lax.fori_loop(0, n_ctx_blk, lambda j, c: update(scores(j, 1), j, c, 1),
                            tuple((m0, a0) for _ in qms))
        res = lax.fori_loop(0, n_lat_pairs, pair, res)
        return [acc[0:HD] * (1.0 / acc[HD:HD + 1]) for (_, acc) in res]

    outs = []
    if diff:
        lv = lam_ref[...]
        lam = (jnp.exp(jnp.sum(lv[0:1] * lv[1:2], axis=1, keepdims=True))
               - jnp.exp(jnp.sum(lv[2:3] * lv[3:4], axis=1, keepdims=True)) + lam_init)
        for hp in range(2):
            qms, vlos = [], []
            for h in (2 * hp, 2 * hp + 1):
                for j in range(2):
                    lo = (2 * h + j) * (HD // 2)
                    qms.append(jnp.where((rowi >= lo) & (rowi < lo + HD // 2), qt, jnp.zeros_like(qt)))
                    vlos.append(h * HD)
            r = run_streams(qms, vlos)
            outs += [r[0] - lam * r[1], r[2] - lam * r[3]]
    else:
        qms, vlos = [], []
        for h in range(4):
            kvh = h // 2
            qh = qt[h * HD:(h + 1) * HD, :]
            z = jnp.zeros_like(qh)
            qms.append(jnp.concatenate([qh, z] if kvh == 0 else [z, qh], axis=0))
            vlos.append(kvh * HD)
        outs = run_streams(qms, vlos)
    o_ref[...] = jnp.concatenate(outs, axis=0).T


def _attn_call(qt, k, vt, lam, B, T, CT, diff, lam_init, name):
    nt = T // TM
    kd = k.shape[1]
    vr = vt.shape[2]
    assert (nt - CT // TM) % (2 * KV_BLKS) == 0
    kern = functools.partial(_attn_kernel, diff=diff, lam_init=lam_init, n_ctx_tiles=CT // TM,
                             n_ctx_blk=CT // TM, n_all_blk=nt)
    score_buf = pltpu.VMEM((4, KV_BLKS * TM, TM), F32)
    return pl.pallas_call(
        kern, grid=(B, nt),
        in_specs=[pl.BlockSpec((1, 1, GW, TM), lambda b, i: (b, i, 0, 0)),
                  pl.BlockSpec((T, kd), lambda b, i: (b, 0)),
                  pl.BlockSpec((1, nt, vr, TM), lambda b, i: (b, 0, 0, 0)),
                  pl.BlockSpec(lam.shape, lambda b, i: (0, 0))],
        out_specs=pl.BlockSpec((TM, GW), lambda b, i: (b * nt + i, 0)),
        out_shape=jax.ShapeDtypeStruct((B * T, GW), F32),
        scratch_shapes=[score_buf, score_buf],
        compiler_params=_cparams(("arbitrary", "arbitrary")), name=name,
    )(qt, k, vt, lam)


def _chunk_masks(reverse):
    row = lax.broadcasted_iota(jnp.int32, (CH, GW), 0)
    lane = lax.broadcasted_iota(jnp.int32, (CH, GW), 1)
    sidx = lane % CH
    eye = row == sidx
    causal = (sidx >= row) if reverse else (sidx <= row)
    strict = (sidx > row) if reverse else (sidx < row)
    r2 = lax.broadcasted_iota(jnp.int32, (GW, GW), 0)
    c2 = lax.broadcasted_iota(jnp.int32, (GW, GW), 1)
    blk = (r2 // CH) == (c2 // CH)
    t1 = lax.broadcasted_iota(jnp.int32, (CH, CH), 0)
    t2 = lax.broadcasted_iota(jnp.int32, (CH, CH), 1)
    tri = ((t2 >= t1) if reverse else (t2 <= t1)).astype(F32)
    return eye, causal, strict, blk, tri


def _bd(y, blk):
    y16 = y.astype(BF16)
    return jnp.where(blk, jnp.concatenate([y16, y16, y16, y16], axis=0), jnp.zeros((), BF16))


def _dot_x3_bd(x, y, blk):
    xh, xl = _split(x)
    yh, yl = _split(y)
    bh, bl = _bd(yh, blk), _bd(yl, blk)
    d = functools.partial(lax.dot_general, dimension_numbers=NN, preferred_element_type=F32)
    return d(xh, bh) + d(xh, bl) + d(xl, bh)


def _row_form(col, eye):
    return _dot_xr(jnp.ones((CH, CH), F32), jnp.where(eye, col, 0.0))


def _cummax(a, reverse):
    n = a.shape[0]
    row = lax.broadcasted_iota(jnp.int32, a.shape, 0)
    k = 1
    while k < n:
        if reverse:
            sh, valid = pltpu.roll(a, n - k, 0), row < n - k
        else:
            sh, valid = pltpu.roll(a, k, 0), row >= k
        a = jnp.maximum(a, jnp.where(valid, sh, -jnp.inf))
        k *= 2
    return a


def _chunk_order(p, nc_ctx, nch, reverse):
    if not reverse:
        return p
    return jnp.where(p < nc_ctx, nc_ctx - 1 - p, nch + nc_ctx - 1 - p)


def _mlstm_kernel(q_ref, k_ref, v_ref, g_ref, bias_ref, ei_ref, ef_ref, h_ref, c_sc, n_sc, m_sc, *, reverse):
    @pl.when(pl.program_id(0) == 0)
    def _():
        c_sc[...] = jnp.zeros_like(c_sc)
        n_sc[...] = jnp.zeros_like(n_sc)
        m_sc[...] = jnp.zeros_like(m_sc)

    eye, causal, _, blk, tri = _chunk_masks(reverse)
    last = 0 if reverse else CH - 1
    R = range(q_ref.shape[0])
    q4 = [q_ref[b] for b in R]
    k4 = [k_ref[b] * (HD ** -0.5) for b in R]
    v4 = [v_ref[b] for b in R]
    gt = [g_ref[b] + bias_ref[...] for b in R]
    i_e = [_dot_xl(gt[b], ei_ref[0]) for b in R]
    f_e = [_dot_xl(gt[b], ef_ref[0]) for b in R]
    logf = [jnp.minimum(f_e[b], 0.0) - jnp.log(1.0 + jnp.exp(-jnp.abs(f_e[b]))) for b in R]
    b_e = [_dot_xr(tri, logf[b]) for b in R]
    m_prev = [m_sc[b, 0:1, :] for b in R]
    a_e = [i_e[b] - b_e[b] for b in R]
    rowa = [_row_form(a_e[b], eye) for b in R]
    inter = [b_e[b] + m_prev[b] for b in R]
    m_t = [jnp.maximum(inter[b], b_e[b] + _cummax(a_e[b], reverse)) for b in R]
    w_inter = [jnp.exp(inter[b] - m_t[b]) for b in R]
    dexp = [jnp.exp(jnp.where(causal, b_e[b] - m_t[b] + rowa[b], -jnp.inf)) for b in R]
    smat = [_dot(q4[b], _bd(k4[b], blk), NT) * dexp[b] for b in R]
    cbd = [c_sc[b] for b in R]
    nbd = [n_sc[b] for b in R]
    qc = [_dot(q4[b], cbd[b]) for b in R]
    qn = [_dot(q4[b], nbd[b]) for b in R]
    sv = [_dot(smat[b], _bd(v4[b], blk)) for b in R]
    ssum = [_dot_xl(smat[b], blk.astype(F32)) for b in R]
    b_last = [b_e[b][last:last + 1, :] for b in R]
    g = [b_last[b] - b_e[b] + i_e[b] for b in R]
    m_new = [jnp.maximum(b_last[b] + m_prev[b], jnp.max(g[b], axis=0, keepdims=True)) for b in R]
    ks = [k4[b] * jnp.exp(g[b] - m_new[b]) for b in R]
    cup = [_dot(ks[b], v4[b], TN) for b in R]
    nup = [_dot(ks[b], jnp.ones((CH, GW), F32), TN) for b in R]
    for b in R:
        num = w_inter[b] * qc[b] + sv[b]
        den = w_inter[b] * qn[b] + ssum[b]
        h_ref[b] = num / jnp.maximum(jnp.abs(den), jnp.exp(-m_t[b]))
        a_prev = jnp.exp(b_last[b] + m_prev[b] - m_new[b])
        c_sc[b] = a_prev * cbd[b] + jnp.where(blk, cup[b], 0.0)
        n_sc[b] = a_prev * nbd[b] + jnp.where(blk, nup[b], 0.0)
        m_sc[b] = jnp.broadcast_to(m_new[b], (8, GW))


def _mlstm_call(U, bias128, e_i, e_f, B, T, CT, reverse):
    nch, ncc = T // CH, CT // CH
    U3 = U.reshape(B, T, U.shape[1])
    cmap = lambda col: (lambda p: (0, _chunk_order(p, ncc, nch, reverse), col))
    d = 1 if reverse else 0
    out = pl.pallas_call(
        functools.partial(_mlstm_kernel, reverse=reverse), grid=(nch,),
        in_specs=[pl.BlockSpec((B, CH, GW), cmap(U_CQ // GW)),
                  pl.BlockSpec((B, CH, GW), cmap(U_CK // GW)),
                  pl.BlockSpec((B, CH, GW), cmap(U_CV // GW)),
                  pl.BlockSpec((B, CH, LANES), cmap(U_G // LANES)),
                  pl.BlockSpec((1, LANES), lambda p: (0, 0)),
                  pl.BlockSpec((1, LANES, GW), lambda p: (d, 0, 0)),
                  pl.BlockSpec((1, LANES, GW), lambda p: (d, 0, 0))],
        out_specs=pl.BlockSpec((B, CH, GW), cmap(0)),
        out_shape=jax.ShapeDtypeStruct((B, T, GW), F32),
        scratch_shapes=[pltpu.VMEM((B, GW, GW), F32), pltpu.VMEM((B, GW, GW), F32), pltpu.VMEM((B, 8, GW), F32)],
        compiler_params=_cparams(("arbitrary",)),
        name="mlstm_bwd" if reverse else "mlstm_fwd",
    )(U3, U3, U3, U3, bias128, e_i, e_f)
    return out.reshape(B * T, GW)


def _gdn_prep_kernel(prev_ref, cur_ref, next_ref, g_ref, conv_ref, g64_ref, e_ref, alog_ref, dtb_ref,
                     q_ref, k_ref, v_ref, bf_ref, gf_ref, bb_ref, gb_ref, *, n_ctx_tiles, nt):
    i = pl.program_id(1)
    at_start = (i == 0) | (i == n_ctx_tiles)
    at_end = (i == n_ctx_tiles - 1) | (i == nt - 1)
    cur = cur_ref[...]
    head = jnp.where(at_start, 0.0, prev_ref[TM - 8:TM, :])
    tail = jnp.where(at_end, 0.0, next_ref[0:8, :])
    xx = jnp.concatenate([head, cur, tail], axis=0)
    cw = conv_ref[...]
    kw = cw.shape[0]
    acc = jnp.zeros_like(cur)
    for j in range(kw):
        s0 = 8 + j - kw // 2
        acc = acc + xx[s0:s0 + TM, :] * cw[j:j + 1, :]
    t = _silu(acc)
    q, k, v = t[:, 0:GW], t[:, GW:2 * GW], t[:, 2 * GW:3 * GW]
    g64 = g64_ref[...] * float(HD)
    q_ref[...] = q * lax.rsqrt(_dot_xl(q * q, g64) + EPS) * (HD ** -0.5)
    k_ref[...] = k * lax.rsqrt(_dot_xl(k * k, g64) + EPS)
    v_ref[...] = v
    gt = g_ref[...]
    for d, (b_out, g_out) in enumerate(((bf_ref, gf_ref), (bb_ref, gb_ref))):
        b_out[...] = _sigmoid(_dot_xl(gt, e_ref[2 * d]))
        a_e = _dot_xl(gt, e_ref[2 * d + 1])
        g_out[...] = -jnp.exp(alog_ref[d:d + 1, :]) * _softplus(a_e + dtb_ref[d:d + 1, :])


def _gdn_prep_call(U, conv_w, g64, e_d, alog_e, dtb_e, B, T, CT):
    nt = T // TM
    w3 = 3 * GW
    row = lambda b, i: (b * nt + i, 0)
    const2 = lambda b, i: (0, 0)
    o = jax.ShapeDtypeStruct((B * T, GW), F32)
    return pl.pallas_call(
        functools.partial(_gdn_prep_kernel, n_ctx_tiles=CT // TM, nt=nt), grid=(B, nt),
        in_specs=[pl.BlockSpec((TM, w3), lambda b, i: (b * nt + jnp.maximum(i - 1, 0), 0)),
                  pl.BlockSpec((TM, w3), row),
                  pl.BlockSpec((TM, w3), lambda b, i: (b * nt + jnp.minimum(i + 1, nt - 1), 0)),
                  pl.BlockSpec((TM, LANES), lambda b, i: (b * nt + i, U_G // LANES)),
                  pl.BlockSpec(conv_w.shape, const2),
                  pl.BlockSpec((GW, GW), const2),
                  pl.BlockSpec(e_d.shape, lambda b, i: (0, 0, 0)),
                  pl.BlockSpec((2, GW), const2), pl.BlockSpec((2, GW), const2)],
        out_specs=[pl.BlockSpec((TM, GW), row)] * 7,
        out_shape=[o] * 7,
        compiler_params=_cparams(("arbitrary", "arbitrary")), name="gdn_prep",
    )(U, U, U, U, conv_w, g64, e_d, alog_e, dtb_e)


def _gdn_kernel(q_ref, k_ref, v_ref, beta_ref, g_ref, o_ref, s_sc, *, reverse):
    @pl.when(pl.program_id(0) == 0)
    def _():
        s_sc[...] = jnp.zeros_like(s_sc)

    eye, causal, strict, blk, tri = _chunk_masks(reverse)
    last = 0 if reverse else CH - 1
    R = range(q_ref.shape[0])
    q4, k4, v4, beta = ([r[b] for b in R] for r in (q_ref, k_ref, v_ref, beta_ref))
    gam = [_dot_xr(tri, g_ref[b]) for b in R]
    rowg = [_row_form(gam[b], eye) for b in R]
    decay = [jnp.exp(jnp.where(causal, gam[b] - rowg[b], -jnp.inf)) for b in R]
    kbeta = [k4[b] * beta[b] for b in R]
    kbd = [_bd(k4[b], blk) for b in R]
    a4 = [jnp.where(strict, _dot(kbeta[b], kbd[b], NT) * decay[b], 0.0) for b in R]
    attn = [_dot(q4[b], kbd[b], NT) * decay[b] for b in R]
    tinv = [jnp.where(eye, 1.0, 0.0) - a4[b] for b in R]
    apow = a4
    for _ in range(5):
        apow = [_dot_x3_bd(apow[b], apow[b], blk) for b in R]
        tinv = [tinv[b] + _dot_x3_bd(tinv[b], apow[b], blk) for b in R]
    u4 = [_dot_x3_bd(tinv[b], v4[b] * beta[b], blk) for b in R]
    w4 = [_dot_x3_bd(tinv[b], kbeta[b] * jnp.exp(gam[b]), blk) for b in R]
    sbd = [s_sc[b] for b in R]
    v_new = [u4[b] - _dot(w4[b], sbd[b]) for b in R]
    o_inter = [_dot(q4[b] * jnp.exp(gam[b]), sbd[b]) for b in R]
    o_intra = [_dot(attn[b], _bd(v_new[b], blk)) for b in R]
    upd = [_dot(k4[b] * jnp.exp(gam[b][last:last + 1, :] - gam[b]), v_new[b], TN) for b in R]
    for b in R:
        o_ref[b] = o_inter[b] + o_intra[b]
        s_sc[b] = sbd[b] * jnp.exp(gam[b][last:last + 1, :]) + jnp.where(blk, upd[b], 0.0)


def _gdn_call(q, k, v, beta, g, B, T, CT, reverse):
    nch, ncc = T // CH, CT // CH
    spec = pl.BlockSpec((B, CH, GW), lambda p: (0, _chunk_order(p, ncc, nch, reverse), 0))
    r3 = lambda a: a.reshape(B, T, GW)
    out = pl.pallas_call(
        functools.partial(_gdn_kernel, reverse=reverse), grid=(nch,),
        in_specs=[spec] * 5, out_specs=spec,
        out_shape=jax.ShapeDtypeStruct((B, T, GW), F32),
        scratch_shapes=[pltpu.VMEM((B, GW, GW), F32)],
        compiler_params=_cparams(("arbitrary",)),
        name="gdn_bwd" if reverse else "gdn_fwd",
    )(r3(q), r3(k), r3(v), r3(beta), r3(g))
    return out.reshape(B * T, GW)


def _post_kernel(x_ref, mod_ref, oa_ref, ob_ref, hcf_ref, hcb_ref, odf_ref, odb_ref, oc_ref, zd_ref,
                 gains_ref, g64_ref, w_ref, xo_ref, *, lam_init):
    gains = gains_ref[...]
    g64 = g64_ref

    oa = oa_ref[...]
    ya = oa * lax.rsqrt(_group_mean(oa * oa, g64) + EPS) * gains[0:1] * (1.0 - lam_init)
    yb = ob_ref[...]
    hc = hcf_ref[...] + hcb_ref[...]
    dc = hc - _group_mean(hc, g64)
    yc = _sigmoid(oc_ref[...]) * (dc * lax.rsqrt(_group_mean(dc * dc, g64) + EPS) * gains[1:2])
    od = odf_ref[...] + odb_ref[...]
    yd = od * lax.rsqrt(_group_mean(od * od, g64) + EPS) * gains[2:3] * _silu(zd_ref[...])
    out = _dot(ya, w_ref[0:GW, :])
    out = out + _dot(yb, w_ref[GW:2 * GW, :])
    out = out + _dot(yc, w_ref[2 * GW:3 * GW, :])
    out = out + _dot(yd, w_ref[3 * GW:4 * GW, :])
    xo_ref[...] = x_ref[...] + mod_ref[0, 0][2:3] * out


def _post_call(X, modtab, oa, ob, hcf, hcb, odf, odb, U, gains, g64, w_out, B, T, lam_init):
    nt = T // TM
    d = X.shape[1]
    row = lambda b, i: (b * nt + i, 0)
    const = lambda b, i: (0, 0)
    gspec = pl.BlockSpec((TM, GW), row)
    return pl.pallas_call(
        functools.partial(_post_kernel, lam_init=lam_init), grid=(B, nt),
        in_specs=[pl.BlockSpec((TM, d), row),
                  pl.BlockSpec((1, 1, 8, d), lambda b, i: (b, jnp.minimum(i, 1), 0, 0)),
                  gspec, gspec, gspec, gspec, gspec, gspec,
                  pl.BlockSpec((TM, GW), lambda b, i: (b * nt + i, U_CO // GW)),
                  pl.BlockSpec((TM, GW), lambda b, i: (b * nt + i, U_DZ // GW)),
                  pl.BlockSpec((8, GW), const), pl.BlockSpec((GW, GW), const),
                  pl.BlockSpec(w_out.shape, const)],
        out_specs=pl.BlockSpec((TM, d), row),
        out_shape=jax.ShapeDtypeStruct(X.shape, F32),
        compiler_params=_cparams(("arbitrary", "arbitrary")), name="out_proj",
    )(X, modtab, oa, ob, hcf, hcb, odf, odb, U, U, gains, g64, w_out)


FF_CHUNK = 256


def _ffn_kernel(x_ref, mod_ref, gain_ref, wg_ref, wu_ref, wd_ref, xo_ref):
    x = x_ref[...]
    mod = mod_ref[0, 0]
    h = _norm_mod(x, gain_ref[...], mod[3:4], mod[4:5]).astype(BF16)
    acc = jnp.zeros(x.shape, F32)
    for f in range(0, wg_ref.shape[1], FF_CHUNK):
        g = lax.dot_general(h, wg_ref[:, f:f + FF_CHUNK], NN, preferred_element_type=F32)
        u = lax.dot_general(h, wu_ref[:, f:f + FF_CHUNK], NN, preferred_element_type=F32)
        acc = acc + _dot(_silu(g) * u, wd_ref[f:f + FF_CHUNK, :])
    xo_ref[...] = x + mod[5:6] * acc


def _ffn_call(X, modtab, gain, wg, wu, wd, nb, tiles_per_b, kind_of_tile):
    d = X.shape[1]
    row = lambda b, i: (b * tiles_per_b + i, 0)
    const = lambda b, i: (0, 0)
    return pl.pallas_call(
        _ffn_kernel, grid=(nb, tiles_per_b),
        in_specs=[pl.BlockSpec((TM, d), row),
                  pl.BlockSpec((1, 1, 8, d), lambda b, i: (b, kind_of_tile(i), 0, 0)),
                  pl.BlockSpec((1, d), const),
                  pl.BlockSpec(wg.shape, const), pl.BlockSpec(wu.shape, const), pl.BlockSpec(wd.shape, const)],
        out_specs=pl.BlockSpec((TM, d), row),
        out_shape=jax.ShapeDtypeStruct(X.shape, F32),
        compiler_params=_cparams(("arbitrary", "arbitrary")), name="ffn_dense",
    )(X, modtab, gain, wg, wu, wd)


def _router_kernel(x_ref, mod_ref, gain_ref, r_ref, h_ref, gates_ref, *, n_exp):
    x = x_ref[...]
    mod = mod_ref[0, 0]
    h = _norm_mod(x, gain_ref[...], mod[3:4], mod[4:5])
    h_ref[...] = h.astype(BF16)
    logits = lax.dot_general(h, r_ref[...], NN, precision=lax.Precision.HIGHEST, preferred_element_type=F32)
    lane = lax.broadcasted_iota(jnp.int32, logits.shape, 1)
    neg = -jnp.inf
    lg = jnp.where(lane < n_exp, logits, neg)
    v1 = jnp.max(lg, axis=1, keepdims=True)
    i1 = jnp.min(jnp.where(lg == v1, lane, LANES), axis=1, keepdims=True)
    lg2 = jnp.where(lane == i1, neg, lg)
    v2 = jnp.max(lg2, axis=1, keepdims=True)
    i2 = jnp.min(jnp.where(lg2 == v2, lane, LANES), axis=1, keepdims=True)
    e2 = jnp.exp(v2 - v1)
    w1 = 1.0 / (1.0 + e2)
    w2 = e2 / (1.0 + e2)
    gates_ref[...] = jnp.where(lane == i1, w1, 0.0) + jnp.where(lane == i2, w2, 0.0)


def _router_call(X, modtab, gain, router128, nb, tiles_per_b, n_exp):
    d = X.shape[1]
    row = lambda b, i: (b * tiles_per_b + i, 0)
    const = lambda b, i: (0, 0)
    return pl.pallas_call(
        functools.partial(_router_kernel, n_exp=n_exp), grid=(nb, tiles_per_b),
        in_specs=[pl.BlockSpec((TM, d), row),
                  pl.BlockSpec((1, 1, 8, d), lambda b, i: (b, 1, 0, 0)),
                  pl.BlockSpec((1, d), const),
                  pl.BlockSpec(router128.shape, const)],
        out_specs=[pl.BlockSpec((TM, d), row), pl.BlockSpec((TM, LANES), row)],
        out_shape=[jax.ShapeDtypeStruct(X.shape, BF16), jax.ShapeDtypeStruct((X.shape[0], LANES), F32)],
        compiler_params=_cparams(("arbitrary", "arbitrary")), name="moe_router",
    )(X, modtab, gain, router128)


MOE_TM = 1024
MOE_C0 = 320
MOE_CH = 256
MOE_TF = 896


def _moe_kernel(x_ref, mod_ref, h_ref, gates_ref, tri_ref, wg_ref, wu_ref, wd_ref, xo_ref,
                acc_sc, rk_sc, rkt_sc, xg0_sc, y0_sc, xg_sc, y_sc, nck_sm):
    e = pl.program_id(2)
    f = pl.program_id(3)
    ne = pl.num_programs(2)
    nf = pl.num_programs(3)
    tm = h_ref.shape[0]
    c0 = xg0_sc.shape[0]

    def onehot(base, rows, rrow):
        r = lax.broadcasted_iota(jnp.int32, (rows, 1), 0).astype(F32) + base
        return jnp.where(r == rrow, 1.0, 0.0).astype(BF16)

    def onehot_t(base, rows, rcol):
        c = lax.broadcasted_iota(jnp.int32, (1, rows), 1).astype(F32) + base
        return jnp.where(rcol == c, 1.0, 0.0).astype(BF16)

    def ffn(xk):
        g = lax.dot_general(xk, wg_ref[0], NN, preferred_element_type=F32)
        u = lax.dot_general(xk, wu_ref[0], NN, preferred_element_type=F32)
        return _dot(_silu(g) * u, wd_ref[0])

    def rest_base(k):
        return (c0 + k * MOE_CH).astype(F32)

    @pl.when((e == 0) & (f == 0))
    def _():
        acc_sc[...] = jnp.zeros_like(acc_sc)
        sel = gates_ref[...] > 0.0
        rank = lax.dot_general(tri_ref[...], jnp.where(sel, 1.0, 0.0).astype(BF16), NN,
                               preferred_element_type=F32)
        rk = jnp.where(sel, rank, -1.0)
        rk_sc[...] = rk
        rkt_sc[...] = rk.T

    @pl.when(f == 0)
    def _():
        rrow = rkt_sc[pl.ds(e, 1), :]
        cnt = jnp.sum(jnp.where(rrow >= 0.0, 1.0, 0.0))
        nck_sm[0] = (jnp.maximum(cnt - c0, 0.0) + (MOE_CH - 1.0)).astype(jnp.int32) // MOE_CH
        h = h_ref[...]
        xg0_sc[...] = lax.dot_general(onehot(0.0, c0, rrow), h, NN, preferred_element_type=F32).astype(BF16)

        def gather(k, carry):
            xg_sc[k] = lax.dot_general(onehot(rest_base(k), MOE_CH, rrow), h, NN,
                                       preferred_element_type=F32).astype(BF16)
            y_sc[k] = jnp.zeros((MOE_CH, y_sc.shape[2]), F32)
            return carry

        lax.fori_loop(0, nck_sm[0], gather, 0)

    y0 = ffn(xg0_sc[...])

    @pl.when(f == 0)
    def _():
        y0_sc[...] = y0

    @pl.when(f > 0)
    def _():
        y0_sc[...] += y0

    def expert(k, carry):
        y_sc[k] += ffn(xg_sc[k])
        return carry

    lax.fori_loop(0, nck_sm[0], expert, 0)

    @pl.when(f == nf - 1)
    def _():
        lane = lax.broadcasted_iota(jnp.int32, (tm, LANES), 1)
        rcol = jnp.sum(jnp.where(lane == e, rk_sc[...], 0.0), axis=1, keepdims=True)
        gcol = jnp.sum(jnp.where(lane == e, gates_ref[...], 0.0), axis=1, keepdims=True)
        acc_sc[...] += gcol * lax.dot_general(onehot_t(0.0, c0, rcol), y0_sc[...].astype(BF16), NN,
                                              preferred_element_type=F32)

        def scatter(k, carry):
            acc_sc[...] += gcol * lax.dot_general(onehot_t(rest_base(k), MOE_CH, rcol), y_sc[k].astype(BF16), NN,
                                                  preferred_element_type=F32)
            return carry

        lax.fori_loop(0, nck_sm[0], scatter, 0)

    @pl.when((e == ne - 1) & (f == nf - 1))
    def _():
        xo_ref[...] = x_ref[...] + mod_ref[0, 0][5:6] * acc_sc[...]


def _moe_call(X, modtab, hb, gates, wg, wu, wd, nb, rows_per_b):
    d = X.shape[1]
    n_exp, _, ff = wg.shape
    tm = min(MOE_TM, rows_per_b)
    tpb = rows_per_b // tm
    c0 = min(MOE_C0, tm)
    nrest = max(-(-(tm - c0) // MOE_CH), 1)
    tri = jnp.asarray(np.tril(np.ones((tm, tm), np.float32), -1), BF16)
    row = lambda b, i, e, f: (b * tpb + i, 0)
    return pl.pallas_call(
        _moe_kernel, grid=(nb, tpb, n_exp, ff // MOE_TF),
        in_specs=[pl.BlockSpec((tm, d), row),
                  pl.BlockSpec((1, 1, 8, d), lambda b, i, e, f: (b, 1, 0, 0)),
                  pl.BlockSpec((tm, d), row),
                  pl.BlockSpec((tm, LANES), row),
                  pl.BlockSpec((tm, tm), lambda b, i, e, f: (0, 0)),
                  pl.BlockSpec((1, d, MOE_TF), lambda b, i, e, f: (e, 0, f)),
                  pl.BlockSpec((1, d, MOE_TF), lambda b, i, e, f: (e, 0, f)),
                  pl.BlockSpec((1, MOE_TF, d), lambda b, i, e, f: (e, f, 0))],
        out_specs=pl.BlockSpec((tm, d), row),
        out_shape=jax.ShapeDtypeStruct(X.shape, F32),
        scratch_shapes=[pltpu.VMEM((tm, d), F32), pltpu.VMEM((tm, LANES), F32), pltpu.VMEM((LANES, tm), F32),
                        pltpu.VMEM((c0, d), BF16), pltpu.VMEM((c0, d), F32),
                        pltpu.VMEM((nrest, MOE_CH, d), BF16), pltpu.VMEM((nrest, MOE_CH, d), F32),
                        pltpu.SMEM((1,), jnp.int32)],
        compiler_params=_cparams(("arbitrary",) * 4), name="moe_experts",
    )(X, modtab, hb, gates, tri, wg, wu, wd)


def _group_mean_matrix(group):
    idx = np.arange(GW)
    return jnp.asarray((idx[:, None] // group == idx[None, :] // group).astype(np.float32) / group, BF16)


def _rope_tables(n_lat, n_ctx, dim, reps):
    t = jnp.arange(n_lat)
    pos = jnp.stack([(t // GRID_W).astype(F32), (t % GRID_W).astype(F32)], axis=1)
    n_freq = dim // 4
    inv_freq = ROPE_THETA ** (-jnp.arange(n_freq, dtype=F32) / n_freq)
    ang = pos[:, :, None] * inv_freq
    cos = jnp.cos(ang)
    sin = jnp.sin(ang)
    cos = jnp.stack([cos, cos], axis=2).reshape(n_lat, dim)
    sin = jnp.stack([-sin, sin], axis=2).reshape(n_lat, dim)
    cos = jnp.concatenate([jnp.ones((n_ctx, dim), F32), cos], axis=0)
    sin = jnp.concatenate([jnp.zeros((n_ctx, dim), F32), sin], axis=0)
    return jnp.tile(cos, (1, reps)), jnp.tile(sin, (1, reps))


def _expand_matrix(col0):
    m = np.zeros((LANES, GW), np.float32)
    for h in range(4):
        m[col0 + h, h * HD:(h + 1) * HD] = 1.0
    return m


_SPLITS = (256, 256, 256, 256, 128, 128, 256, 256, 256, 256, 16, 768, 256, 16)


def _reorder_w_in(w):
    offs = np.concatenate([[0], np.cumsum(_SPLITS)])
    seg = [w[:, offs[i]:offs[i + 1]] for i in range(len(_SPLITS))]
    aq, ak, av, bq, bk, bv, cq, ck, cv, co, cg, dqkv, dz, dg = seg
    pad = jnp.zeros((w.shape[0], LANES - 32), w.dtype)
    return jnp.concatenate([aq, ak, av, bq, bk, bv, dqkv, cq, ck, cv, co, dz, cg, dg, pad], axis=1).astype(BF16)


def kernel(x, c, ctx, c_ctx, w_mod, b_mod, norm1, norm2, w_in, w_out, diff_qk_gain, diff_lambda, diff_subln,
           gqa_qk_gain, mlstm_gate_bias, mlstm_norm, gdn_conv, gdn_a_log, gdn_dt_bias, gdn_norm,
           ffn_w_gate, ffn_w_up, ffn_w_down, moe_router, moe_w_gate, moe_w_up, moe_w_down):
    B, S, D = x.shape
    CT = ctx.shape[1]
    T = CT + S
    depth = w_in.shape[0]
    assert CT % TM == 0 and S % TM == 0 and S % GRID_W == 0 and B + 1 <= 8

    g32 = _group_mean_matrix(HD // 2)
    g64 = _group_mean_matrix(HD)
    cosa, sina = _rope_tables(S, CT, HD // 2, 8)
    cosb, sinb = _rope_tables(S, CT, HD, 4)
    e_i = jnp.asarray(np.stack([_expand_matrix(0), _expand_matrix(8)]), F32)
    e_f = jnp.asarray(np.stack([_expand_matrix(4), _expand_matrix(12)]), F32)
    e_d = jnp.asarray(np.stack([_expand_matrix(16 + 4 * j) for j in range(4)]), F32)

    cvec = jnp.zeros((8, D), F32).at[:B].set(c).at[B].set(c_ctx)
    mod_all = _mod_call(cvec, w_mod, b_mod).reshape(depth, 8, 6, D)

    X = jnp.concatenate([ctx, x], axis=1).reshape(B * T, D)
    nb_rows = T
    for li in range(depth):
        with_ctx = li < depth - 1
        lam_init = 0.8 - 0.6 * math.exp(-0.3 * li)
        m = mod_all[li]
        modtab = jnp.stack([jnp.broadcast_to(m[B], (B, 6, D)), m[:B]], axis=1)
        modtab = jnp.pad(modtab, ((0, 0), (0, 0), (0, 2), (0, 0)))

        qkg = jnp.zeros((8, GW), F32)
        qkg = qkg.at[0].set(jnp.tile(diff_qk_gain[li, 0], 8)).at[1].set(jnp.tile(diff_qk_gain[li, 1], 8))
        qkg = qkg.at[2].set(jnp.tile(gqa_qk_gain[li, 0], 4)).at[3].set(jnp.tile(gqa_qk_gain[li, 1], 4))
        qta, ka, vta, qtb, kb, vtb, U = _kin_call(
            X, modtab, norm1[li][None, :], _reorder_w_in(w_in[li]), g32, g64, qkg, cosa, sina, cosb, sinb, B, T)

        oa = _attn_call(qta, ka, vta, diff_lambda[li], B, T, CT, True, lam_init, "attn_diff")
        ob = _attn_call(qtb, kb, vtb, diff_lambda[li], B, T, CT, False, lam_init, "attn_gqa")

        bias128 = jnp.zeros((1, LANES), F32).at[0, :16].set(mlstm_gate_bias[li].reshape(-1))
        hcf = _mlstm_call(U, bias128, e_i, e_f, B, T, CT, False)
        hcb = _mlstm_call(U, bias128, e_i, e_f, B, T, CT, True)

        alog_e = jnp.repeat(gdn_a_log[li], HD, axis=1)
        dtb_e = jnp.repeat(gdn_dt_bias[li], HD, axis=1)
        qd, kd, vd, bf, gf, bb, gb = _gdn_prep_call(U, gdn_conv[li], g64, e_d, alog_e, dtb_e, B, T, CT)
        odf = _gdn_call(qd, kd, vd, bf, gf, B, T, CT, False)
        odb = _gdn_call(qd, kd, vd, bb, gb, B, T, CT, True)

        gains = jnp.zeros((8, GW), F32)
        gains = gains.at[0].set(jnp.tile(diff_subln[li], 4)).at[1].set(mlstm_norm[li])
        gains = gains.at[2].set(jnp.tile(gdn_norm[li], 4))
        X = _post_call(X, modtab, oa, ob, hcf, hcb, odf, odb, U, gains, g64, w_out[li].astype(BF16), B, T, lam_init)

        if with_ctx:
            tiles_per_b, kind = T // TM, (lambda i: jnp.minimum(i, 1))
        else:
            X = X.reshape(B, T, D)[:, CT:].reshape(B * S, D)
            nb_rows = S
            tiles_per_b, kind = S // TM, (lambda i: 1)
        j = li // 2
        if li % 2 == 0:
            X = _ffn_call(X, modtab, norm2[li][None, :], ffn_w_gate[j].astype(BF16), ffn_w_up[j].astype(BF16),
                          ffn_w_down[j].astype(BF16), B, tiles_per_b, kind)
        else:
            assert not with_ctx
            n_exp = moe_router.shape[2]
            router128 = jnp.pad(moe_router[j], ((0, 0), (0, LANES - n_exp)))
            hb, gates = _router_call(X, modtab, norm2[li][None, :], router128, B, tiles_per_b, n_exp)
            X = _moe_call(X, modtab, hb, gates, moe_w_gate[j].astype(BF16), moe_w_up[j].astype(BF16),
                          moe_w_down[j].astype(BF16), B, nb_rows)
    return X.reshape(B, nb_rows, D)[:, nb_rows - S:]
```

```python
import functools
import math

import numpy as np
import jax
import jax.numpy as jnp
from jax import lax
from jax.experimental import pallas as pl
from jax.experimental.pallas import tpu as pltpu

F32 = jnp.float32
BF16 = jnp.bfloat16
EPS = 1e-6
ROPE_THETA = 10000.0
GRID_W = 64
LOG2E = 1.4426950408889634

TM = 256
CH = 64
HD = 64
GW = 256
LANES = 128
KV_BLKS = 2
V_ONES = 16
VMEM_LIMIT = 56 * 1024 * 1024

NN = (((1,), (0,)), ((), ()))
NT = (((1,), (1,)), ((), ()))
TN = (((0,), (0,)), ((), ()))


def _dot(a, b, dims=NN):
    return lax.dot_general(a.astype(BF16), b.astype(BF16), dims, preferred_element_type=F32)


def _split(a):
    hi = a.astype(BF16)
    lo = (a - hi.astype(F32)).astype(BF16)
    return hi, lo


def _dot_xl(a, b_exact, dims=NN):
    hi, lo = _split(a)
    bb = b_exact.astype(BF16)
    return (lax.dot_general(hi, bb, dims, preferred_element_type=F32)
            + lax.dot_general(lo, bb, dims, preferred_element_type=F32))


def _dot_xr(a_exact, b, dims=NN):
    hi, lo = _split(b)
    aa = a_exact.astype(BF16)
    return (lax.dot_general(aa, hi, dims, preferred_element_type=F32)
            + lax.dot_general(aa, lo, dims, preferred_element_type=F32))


def _dot_x3(a, b, dims=NN):
    ah, al = _split(a)
    bh, bl = _split(b)
    d = functools.partial(lax.dot_general, dimension_numbers=dims, preferred_element_type=F32)
    return d(ah, bh) + d(ah, bl) + d(al, bh)


def _sigmoid(x):
    return 1.0 / (1.0 + jnp.exp(-x))


def _silu(x):
    return x * _sigmoid(x)


def _softplus(x):
    return jnp.maximum(x, 0.0) + jnp.log(1.0 + jnp.exp(-jnp.abs(x)))


def _cparams(sem, vmem=VMEM_LIMIT):
    return pltpu.CompilerParams(dimension_semantics=sem, vmem_limit_bytes=vmem)


def _mod_kernel(c_ref, w_ref, b_ref, o_ref):
    cv = c_ref[...]
    s = _silu(cv)
    o_ref[0] = lax.dot_general(s, w_ref[0], NN, precision=lax.Precision.HIGHEST,
                               preferred_element_type=F32) + b_ref[0]


def _mod_call(cvec, w_mod, b_mod):
    depth, d, n6 = w_mod.shape
    rows = cvec.shape[0]
    tn = 1536
    return pl.pallas_call(
        _mod_kernel,
        grid=(depth, n6 // tn),
        in_specs=[pl.BlockSpec((rows, d), lambda l, j: (0, 0)),
                  pl.BlockSpec((1, d, tn), lambda l, j: (l, 0, j)),
                  pl.BlockSpec((1, 1, tn), lambda l, j: (l, 0, j))],
        out_specs=pl.BlockSpec((1, rows, tn), lambda l, j: (l, 0, j)),
        out_shape=jax.ShapeDtypeStruct((depth, rows, n6), F32),
        compiler_params=_cparams(("arbitrary", "arbitrary")),
        name="adaln_mod",
    )(cvec, w_mod, b_mod.reshape(depth, 1, n6))


def _norm_mod(x, gain, shift, scale):
    ms = jnp.mean(x * x, axis=-1, keepdims=True)
    return (x * lax.rsqrt(ms + EPS) * gain) * (1.0 + scale) + shift


def _group_mean(v, g_ref):
    return _dot_xl(v, g_ref[...])


def _rope(xn, cos, sin, half):
    w = xn.shape[1]
    lane = lax.broadcasted_iota(jnp.int32, xn.shape, 1)
    first = (lane % (2 * half)) < half
    xs = jnp.where(first, pltpu.roll(xn, w - half, 1), pltpu.roll(xn, half, 1))
    return xn * cos + xs * sin


A_W = 3 * GW
B_W = GW + 2 * 128
AB_W = A_W + B_W
U_W = 2176
U_DQKV, U_CQ, U_CK, U_CV, U_CO, U_DZ, U_G = 0, 768, 1024, 1280, 1536, 1792, 2048


def _kin_kernel(x_ref, mod_ref, gain_ref, w_ref, g32_ref, g64_ref, qkg_ref,
                cosa_ref, sina_ref, cosb_ref, sinb_ref,
                qta_ref, ka_ref, vta_ref, qtb_ref, kb_ref, vtb_ref, u_ref):
    x = x_ref[...]
    mod = mod_ref[0, 0]
    h = _norm_mod(x, gain_ref[...], mod[0:1], mod[1:2]).astype(BF16)

    ya = lax.dot_general(h, w_ref[:, 0:A_W], NN, preferred_element_type=F32)
    qkg = qkg_ref[...]
    ca, sa = cosa_ref[...], sina_ref[...]

    def prep(v, g_ref, gain, cos, sin, half):
        vn = v * lax.rsqrt(_group_mean(v * v, g_ref) + EPS) * gain
        return _rope(vn, cos, sin, half)

    qa = prep(ya[:, 0:GW], g32_ref, qkg[0:1], ca, sa, 8) * ((HD // 2) ** -0.5 * LOG2E)
    ka = prep(ya[:, GW:2 * GW], g32_ref, qkg[1:2], ca, sa, 8)
    qta_ref[0, 0] = qa.T.astype(BF16)
    ka_ref[...] = ka.astype(BF16)
    vta_ref[0, 0] = ya[:, 2 * GW:3 * GW].T.astype(BF16)

    yb = lax.dot_general(h, w_ref[:, A_W:AB_W], NN, preferred_element_type=F32)
    cb, sb = cosb_ref[...], sinb_ref[...]
    qb = prep(yb[:, 0:GW], g64_ref, qkg[2:3], cb, sb, 16) * (HD ** -0.5 * LOG2E)
    kb = prep(yb[:, GW:GW + 128], g64_ref.at[0:128, 0:128], qkg[3:4, 0:128], cb[:, 0:128], sb[:, 0:128], 16)
    qtb_ref[0, 0] = qb.T.astype(BF16)
    kb_ref[...] = kb.astype(BF16)
    vtb_ref[0, 0] = yb[:, GW + 128:GW + 256].T.astype(BF16)

    u_ref[...] = lax.dot_general(h, w_ref[:, AB_W:AB_W + U_W], NN, preferred_element_type=F32)


def _kin_call(X, modtab, gain, w_r, g32, g64, qkg, cosa, sina, cosb, sinb, B, T):
    nt = T // TM
    d = X.shape[1]
    wcols = w_r.shape[1]
    row = lambda b, i: (b * nt + i, 0)
    const = lambda b, i: (0, 0)
    pos = lambda b, i: (i, 0)
    tmap = lambda b, i: (b, i, 0, 0)
    in_specs = [
        pl.BlockSpec((TM, d), row),
        pl.BlockSpec((1, 1, 8, d), lambda b, i: (b, jnp.minimum(i, 1), 0, 0)),
        pl.BlockSpec((1, d), const),
        pl.BlockSpec((d, wcols), const),
        pl.BlockSpec((GW, GW), const),
        pl.BlockSpec((GW, GW), const),
        pl.BlockSpec((8, GW), const),
        pl.BlockSpec((TM, GW), pos), pl.BlockSpec((TM, GW), pos),
        pl.BlockSpec((TM, GW), pos), pl.BlockSpec((TM, GW), pos),
    ]
    out_specs = [
        pl.BlockSpec((1, 1, GW, TM), tmap), pl.BlockSpec((TM, GW), row), pl.BlockSpec((1, 1, GW, TM), tmap),
        pl.BlockSpec((1, 1, GW, TM), tmap), pl.BlockSpec((TM, 128), row), pl.BlockSpec((1, 1, 128, TM), tmap),
        pl.BlockSpec((TM, U_W), row),
    ]
    out_shape = [
        jax.ShapeDtypeStruct((B, nt, GW, TM), BF16), jax.ShapeDtypeStruct((B * T, GW), BF16),
        jax.ShapeDtypeStruct((B, nt, GW, TM), BF16),
        jax.ShapeDtypeStruct((B, nt, GW, TM), BF16), jax.ShapeDtypeStruct((B * T, 128), BF16),
        jax.ShapeDtypeStruct((B, nt, 128, TM), BF16),
        jax.ShapeDtypeStruct((B * T, U_W), F32),
    ]
    return pl.pallas_call(
        _kin_kernel, grid=(B, nt), in_specs=in_specs, out_specs=out_specs, out_shape=out_shape,
        compiler_params=_cparams(("arbitrary", "arbitrary")), name="in_proj",
    )(X, modtab, gain, w_r, g32, g64, qkg, cosa, sina, cosb, sinb)


def _attn_kernel(qt_ref, k_ref, vt_ref, lam_ref, o_ref, sa_sc, sb_sc, *,
                 diff, lam_init, n_ctx_tiles, n_ctx_blk, n_all_blk):
    i = pl.program_id(1)
    n_lat_pairs = jnp.where(i < n_ctx_tiles, 0, (n_all_blk - n_ctx_blk) // (2 * KV_BLKS))
    qt = qt_ref[0, 0]
    kd = k_ref.shape[1]
    rowi = lax.broadcasted_iota(jnp.int32, (kd, TM), 0)

    def run_streams(qms, vlos):
        def scores(j0, nblk):
            off = pl.multiple_of(j0 * TM, TM)
            kb = k_ref[pl.ds(off, nblk * TM), :]
            return [lax.dot_general(kb, qm, NN, preferred_element_type=F32) for qm in qms]

        def update(ss, j0, carry, nblk):
            stats = []
            for (m, _), s in zip(carry, ss):
                m_new = jnp.maximum(m, jnp.max(s, axis=0, keepdims=True))
                stats.append((m_new, jnp.exp2(m - m_new), jnp.exp2((s - m_new).astype(BF16))))
            ones = jnp.ones((V_ONES, nblk * TM), BF16)
            new = []
            for (_, acc), (m_new, alpha, p), vlo in zip(carry, stats, vlos):
                vb = jnp.concatenate([vt_ref[0, j0 + t, vlo:vlo + HD, :] for t in range(nblk)], axis=1)
                acc = alpha * acc + lax.dot_general(jnp.concatenate([vb, ones], axis=0), p, NN,
                                                    preferred_element_type=F32)
                new.append((m_new, acc))
            return tuple(new)

        def scores_into(buf, j0):
            for si, s in enumerate(scores(j0, KV_BLKS)):
                buf[si] = s

        def pair(it, carry):
            ja = n_ctx_blk + it * (2 * KV_BLKS)
            jb = ja + KV_BLKS
            jn = jnp.minimum(jb + KV_BLKS, n_all_blk - KV_BLKS)
            scores_into(sb_sc, jb)
            carry = update([sa_sc[si] for si in range(len(qms))], ja, carry, KV_BLKS)
            scores_into(sa_sc, jn)
            return update([sb_sc[si] for si in range(len(qms))], jb, carry, KV_BLKS)

        m0 = jnp.full((1, TM), -jnp.inf, F32)
        a0 = jnp.zeros((HD + V_ONES, TM), F32)
        scores_into(sa_sc, n_ctx_blk)
        res = lax.fori_loop(0, n_ctx_blk, lambda j, c: update(scores(j, 1), j, c, 1),
                            tuple((m0, a0) for _ in qms))
        res = lax.fori_loop(0, n_lat_pairs, pair, res)
        return [acc[0:HD] * (1.0 / acc[HD:HD + 1]) for (_, acc) in res]

    outs = []
    if diff:
        lv = lam_ref[...]
        lam = (jnp.exp(jnp.sum(lv[0:1] * lv[1:2], axis=1, keepdims=True))
               - jnp.exp(jnp.sum(lv[2:3] * lv[3:4], axis=1, keepdims=True)) + lam_init)
        for hp in range(2):
            qms, vlos = [], []
            for h in (2 * hp, 2 * hp + 1):
                for j in range(2):
                    lo = (2 * h + j) * (HD // 2)
                    qms.append(jnp.where((rowi >= lo) & (rowi < lo + HD // 2), qt, jnp.zeros_like(qt)))
                    vlos.append(h * HD)
            r = run_streams(qms, vlos)
            outs += [r[0] - lam * r[1], r[2] - lam * r[3]]
    else:
        qms, vlos = [], []
        for h in range(4):
            kvh = h // 2
            qh = qt[h * HD:(h + 1) * HD, :]
            z = jnp.zeros_like(qh)
            qms.append(jnp.concatenate([qh, z] if kvh == 0 else [z, qh], axis=0))
            vlos.append(kvh * HD)
        outs = run_streams(qms, vlos)
    o_ref[...] = jnp.concatenate(outs, axis=0).T


def _attn_call(qt, k, vt, lam, B, T, CT, diff, lam_init, name):
    nt = T // TM
    kd = k.shape[1]
    vr = vt.shape[2]
    assert (nt - CT // TM) % (2 * KV_BLKS) == 0
    kern = functools.partial(_attn_kernel, diff=diff, lam_init=lam_init, n_ctx_tiles=CT // TM,
                             n_ctx_blk=CT // TM, n_all_blk=nt)
    score_buf = pltpu.VMEM((4, KV_BLKS * TM, TM), F32)
    return pl.pallas_call(
        kern, grid=(B, nt),
        in_specs=[pl.BlockSpec((1, 1, GW, TM), lambda b, i: (b, i, 0, 0)),
                  pl.BlockSpec((T, kd), lambda b, i: (b, 0)),
                  pl.BlockSpec((1, nt, vr, TM), lambda b, i: (b, 0, 0, 0)),
                  pl.BlockSpec(lam.shape, lambda b, i: (0, 0))],
        out_specs=pl.BlockSpec((TM, GW), lambda b, i: (b * nt + i, 0)),
        out_shape=jax.ShapeDtypeStruct((B * T, GW), F32),
        scratch_shapes=[score_buf, score_buf],
        compiler_params=_cparams(("arbitrary", "arbitrary")), name=name,
    )(qt, k, vt, lam)


def _chunk_masks(reverse):
    row = lax.broadcasted_iota(jnp.int32, (CH, GW), 0)
    lane = lax.broadcasted_iota(jnp.int32, (CH, GW), 1)
    sidx = lane % CH
    eye = row == sidx
    causal = (sidx >= row) if reverse else (sidx <= row)
    strict = (sidx > row) if reverse else (sidx < row)
    r2 = lax.broadcasted_iota(jnp.int32, (GW, GW), 0)
    c2 = lax.broadcasted_iota(jnp.int32, (GW, GW), 1)
    blk = (r2 // CH) == (c2 // CH)
    t1 = lax.broadcasted_iota(jnp.int32, (CH, CH), 0)
    t2 = lax.broadcasted_iota(jnp.int32, (CH, CH), 1)
    tri = ((t2 >= t1) if reverse else (t2 <= t1)).astype(F32)
    return eye, causal, strict, blk, tri


def _bd(y, blk):
    y16 = y.astype(BF16)
    return jnp.where(blk, jnp.concatenate([y16, y16, y16, y16], axis=0), jnp.zeros((), BF16))


def _dot_x3_bd(x, y, blk):
    xh, xl = _split(x)
    yh, yl = _split(y)
    bh, bl = _bd(yh, blk), _bd(yl, blk)
    d = functools.partial(lax.dot_general, dimension_numbers=NN, preferred_element_type=F32)
    n = x.shape[0]
    both = d(jnp.concatenate([xh, xl], axis=0), bh)
    return both[0:n] + both[n:2 * n] + d(xh, bl)


def _row_form(col, eye):
    return _dot_xr(jnp.ones((CH, CH), F32), jnp.where(eye, col, 0.0))


def _cummax(a, reverse):
    n = a.shape[0]
    row = lax.broadcasted_iota(jnp.int32, a.shape, 0)
    k = 1
    while k < n:
        if reverse:
            sh, valid = pltpu.roll(a, n - k, 0), row < n - k
        else:
            sh, valid = pltpu.roll(a, k, 0), row >= k
        a = jnp.maximum(a, jnp.where(valid, sh, -jnp.inf))
        k *= 2
    return a


def _chunk_order(p, nc_ctx, nch, reverse):
    if not reverse:
        return p
    return jnp.where(p < nc_ctx, nc_ctx - 1 - p, nch + nc_ctx - 1 - p)


def _mlstm_kernel(q_ref, k_ref, v_ref, g_ref, bias_ref, ei_ref, ef_ref, h_ref, c_sc, n_sc, m_sc, *, reverse):
    @pl.when(pl.program_id(0) == 0)
    def _():
        c_sc[...] = jnp.zeros_like(c_sc)
        n_sc[...] = jnp.zeros_like(n_sc)
        m_sc[...] = jnp.zeros_like(m_sc)

    eye, causal, _, blk, tri = _chunk_masks(reverse)
    last = 0 if reverse else CH - 1
    R = range(q_ref.shape[0])
    q4 = [q_ref[b] for b in R]
    k4 = [k_ref[b] * (HD ** -0.5) for b in R]
    v4 = [v_ref[b] for b in R]
    gt = [g_ref[b] + bias_ref[...] for b in R]
    i_e = [_dot_xl(gt[b], ei_ref[0]) for b in R]
    f_e = [_dot_xl(gt[b], ef_ref[0]) for b in R]
    logf = [jnp.minimum(f_e[b], 0.0) - jnp.log(1.0 + jnp.exp(-jnp.abs(f_e[b]))) for b in R]
    b_e = [_dot_xr(tri, logf[b]) for b in R]
    m_prev = [m_sc[b, 0:1, :] for b in R]
    a_e = [i_e[b] - b_e[b] for b in R]
    rowa = [_row_form(a_e[b], eye) for b in R]
    inter = [b_e[b] + m_prev[b] for b in R]
    m_t = [jnp.maximum(inter[b], b_e[b] + _cummax(a_e[b], reverse)) for b in R]
    w_inter = [jnp.exp(inter[b] - m_t[b]) for b in R]
    dexp = [jnp.exp(jnp.where(causal, b_e[b] - m_t[b] + rowa[b], -jnp.inf)) for b in R]
    smat = [_dot(q4[b], _bd(k4[b], blk), NT) * dexp[b] for b in R]
    cbd = [c_sc[b] for b in R]
    nbd = [n_sc[b] for b in R]
    qc = [_dot(q4[b], cbd[b]) for b in R]
    qn = [_dot(q4[b], nbd[b]) for b in R]
    sv = [_dot(smat[b], _bd(v4[b], blk)) for b in R]
    ssum = [_dot_xl(smat[b], blk.astype(F32)) for b in R]
    b_last = [b_e[b][last:last + 1, :] for b in R]
    g = [b_last[b] - b_e[b] + i_e[b] for b in R]
    m_new = [jnp.maximum(b_last[b] + m_prev[b], jnp.max(g[b], axis=0, keepdims=True)) for b in R]
    ks = [k4[b] * jnp.exp(g[b] - m_new[b]) for b in R]
    cup = [_dot(ks[b], v4[b], TN) for b in R]
    nup = [_dot(ks[b], jnp.ones((CH, GW), F32), TN) for b in R]
    for b in R:
        num = w_inter[b] * qc[b] + sv[b]
        den = w_inter[b] * qn[b] + ssum[b]
        h_ref[b] = num / jnp.maximum(jnp.abs(den), jnp.exp(-m_t[b]))
        a_prev = jnp.exp(b_last[b] + m_prev[b] - m_new[b])
        c_sc[b] = a_prev * cbd[b] + jnp.where(blk, cup[b], 0.0)
        n_sc[b] = a_prev * nbd[b] + jnp.where(blk, nup[b], 0.0)
        m_sc[b] = jnp.broadcast_to(m_new[b], (8, GW))


def _mlstm_call(U, bias128, e_i, e_f, B, T, CT, reverse):
    nch, ncc = T // CH, CT // CH
    U3 = U.reshape(B, T, U.shape[1])
    cmap = lambda col: (lambda p: (0, _chunk_order(p, ncc, nch, reverse), col))
    d = 1 if reverse else 0
    out = pl.pallas_call(
        functools.partial(_mlstm_kernel, reverse=reverse), grid=(nch,),
        in_specs=[pl.BlockSpec((B, CH, GW), cmap(U_CQ // GW)),
                  pl.BlockSpec((B, CH, GW), cmap(U_CK // GW)),
                  pl.BlockSpec((B, CH, GW), cmap(U_CV // GW)),
                  pl.BlockSpec((B, CH, LANES), cmap(U_G // LANES)),
                  pl.BlockSpec((1, LANES), lambda p: (0, 0)),
                  pl.BlockSpec((1, LANES, GW), lambda p: (d, 0, 0)),
                  pl.BlockSpec((1, LANES, GW), lambda p: (d, 0, 0))],
        out_specs=pl.BlockSpec((B, CH, GW), cmap(0)),
        out_shape=jax.ShapeDtypeStruct((B, T, GW), F32),
        scratch_shapes=[pltpu.VMEM((B, GW, GW), F32), pltpu.VMEM((B, GW, GW), F32), pltpu.VMEM((B, 8, GW), F32)],
        compiler_params=_cparams(("arbitrary",)),
        name="mlstm_bwd" if reverse else "mlstm_fwd",
    )(U3, U3, U3, U3, bias128, e_i, e_f)
    return out.reshape(B * T, GW)


def _gdn_prep_kernel(prev_ref, cur_ref, next_ref, g_ref, conv_ref, g64_ref, e_ref, alog_ref, dtb_ref,
                     q_ref, k_ref, v_ref, bf_ref, gf_ref, bb_ref, gb_ref, *, n_ctx_tiles, nt):
    i = pl.program_id(1)
    at_start = (i == 0) | (i == n_ctx_tiles)
    at_end = (i == n_ctx_tiles - 1) | (i == nt - 1)
    cur = cur_ref[...]
    head = jnp.where(at_start, 0.0, prev_ref[TM - 8:TM, :])
    tail = jnp.where(at_end, 0.0, next_ref[0:8, :])
    xx = jnp.concatenate([head, cur, tail], axis=0)
    cw = conv_ref[...]
    kw = cw.shape[0]
    acc = jnp.zeros_like(cur)
    for j in range(kw):
        s0 = 8 + j - kw // 2
        acc = acc + xx[s0:s0 + TM, :] * cw[j:j + 1, :]
    t = _silu(acc)
    q, k, v = t[:, 0:GW], t[:, GW:2 * GW], t[:, 2 * GW:3 * GW]
    g64 = g64_ref[...] * float(HD)
    q_ref[...] = q * lax.rsqrt(_dot_xl(q * q, g64) + EPS) * (HD ** -0.5)
    k_ref[...] = k * lax.rsqrt(_dot_xl(k * k, g64) + EPS)
    v_ref[...] = v
    gt = g_ref[...]
    for d, (b_out, g_out) in enumerate(((bf_ref, gf_ref), (bb_ref, gb_ref))):
        b_out[...] = _sigmoid(_dot_xl(gt, e_ref[2 * d]))
        a_e = _dot_xl(gt, e_ref[2 * d + 1])
        g_out[...] = -jnp.exp(alog_ref[d:d + 1, :]) * _softplus(a_e + dtb_ref[d:d + 1, :])


def _gdn_prep_call(U, conv_w, g64, e_d, alog_e, dtb_e, B, T, CT):
    nt = T // TM
    w3 = 3 * GW
    row = lambda b, i: (b * nt + i, 0)
    const2 = lambda b, i: (0, 0)
    o = jax.ShapeDtypeStruct((B * T, GW), F32)
    return pl.pallas_call(
        functools.partial(_gdn_prep_kernel, n_ctx_tiles=CT // TM, nt=nt), grid=(B, nt),
        in_specs=[pl.BlockSpec((TM, w3), lambda b, i: (b * nt + jnp.maximum(i - 1, 0), 0)),
                  pl.BlockSpec((TM, w3), row),
                  pl.BlockSpec((TM, w3), lambda b, i: (b * nt + jnp.minimum(i + 1, nt - 1), 0)),
                  pl.BlockSpec((TM, LANES), lambda b, i: (b * nt + i, U_G // LANES)),
                  pl.BlockSpec(conv_w.shape, const2),
                  pl.BlockSpec((GW, GW), const2),
                  pl.BlockSpec(e_d.shape, lambda b, i: (0, 0, 0)),
                  pl.BlockSpec((2, GW), const2), pl.BlockSpec((2, GW), const2)],
        out_specs=[pl.BlockSpec((TM, GW), row)] * 7,
        out_shape=[o] * 7,
        compiler_params=_cparams(("arbitrary", "arbitrary")), name="gdn_prep",
    )(U, U, U, U, conv_w, g64, e_d, alog_e, dtb_e)


def _gdn_kernel(q_ref, k_ref, v_ref, beta_ref, g_ref, o_ref, s_sc, *, reverse):
    @pl.when(pl.program_id(0) == 0)
    def _():
        s_sc[...] = jnp.zeros_like(s_sc)

    eye, causal, strict, blk, tri = _chunk_masks(reverse)
    last = 0 if reverse else CH - 1
    R = range(q_ref.shape[0])
    q4, k4, v4, beta = ([r[b] for b in R] for r in (q_ref, k_ref, v_ref, beta_ref))
    gam = [_dot_xr(tri, g_ref[b]) for b in R]
    rowg = [_row_form(gam[b], eye) for b in R]
    decay = [jnp.exp(jnp.where(causal, gam[b] - rowg[b], -jnp.inf)) for b in R]
    kbeta = [k4[b] * beta[b] for b in R]
    kbd = [_bd(k4[b], blk) for b in R]
    a4 = [jnp.where(strict, _dot(kbeta[b], kbd[b], NT) * decay[b], 0.0) for b in R]
    attn = [_dot(q4[b], kbd[b], NT) * decay[b] for b in R]
    tinv = [jnp.where(eye, 1.0, 0.0) - a4[b] for b in R]
    apow = a4
    for _ in range(5):
        apow = [_dot_x3_bd(apow[b], apow[b], blk) for b in R]
        tinv = [tinv[b] + _dot_x3_bd(tinv[b], apow[b], blk) for b in R]
    u4 = [_dot_x3_bd(tinv[b], v4[b] * beta[b], blk) for b in R]
    w4 = [_dot_x3_bd(tinv[b], kbeta[b] * jnp.exp(gam[b]), blk) for b in R]
    sbd = [s_sc[b] for b in R]
    v_new = [u4[b] - _dot(w4[b], sbd[b]) for b in R]
    o_inter = [_dot(q4[b] * jnp.exp(gam[b]), sbd[b]) for b in R]
    o_intra = [_dot(attn[b], _bd(v_new[b], blk)) for b in R]
    upd = [_dot(k4[b] * jnp.exp(gam[b][last:last + 1, :] - gam[b]), v_new[b], TN) for b in R]
    for b in R:
        o_ref[b] = o_inter[b] + o_intra[b]
        s_sc[b] = sbd[b] * jnp.exp(gam[b][last:last + 1, :]) + jnp.where(blk, upd[b], 0.0)


def _gdn_call(q, k, v, beta, g, B, T, CT, reverse):
    nch, ncc = T // CH, CT // CH
    spec = pl.BlockSpec((B, CH, GW), lambda p: (0, _chunk_order(p, ncc, nch, reverse), 0))
    r3 = lambda a: a.reshape(B, T, GW)
    out = pl.pallas_call(
        functools.partial(_gdn_kernel, reverse=reverse), grid=(nch,),
        in_specs=[spec] * 5, out_specs=spec,
        out_shape=jax.ShapeDtypeStruct((B, T, GW), F32),
        scratch_shapes=[pltpu.VMEM((B, GW, GW), F32)],
        compiler_params=_cparams(("arbitrary",)),
        name="gdn_bwd" if reverse else "gdn_fwd",
    )(r3(q), r3(k), r3(v), r3(beta), r3(g))
    return out.reshape(B * T, GW)


def _post_kernel(x_ref, mod_ref, oa_ref, ob_ref, hcf_ref, hcb_ref, odf_ref, odb_ref, oc_ref, zd_ref,
                 gains_ref, g64_ref, w_ref, xo_ref, *, lam_init):
    gains = gains_ref[...]
    g64 = g64_ref

    oa = oa_ref[...]
    ya = oa * lax.rsqrt(_group_mean(oa * oa, g64) + EPS) * gains[0:1] * (1.0 - lam_init)
    yb = ob_ref[...]
    hc = hcf_ref[...] + hcb_ref[...]
    dc = hc - _group_mean(hc, g64)
    yc = _sigmoid(oc_ref[...]) * (dc * lax.rsqrt(_group_mean(dc * dc, g64) + EPS) * gains[1:2])
    od = odf_ref[...] + odb_ref[...]
    yd = od * lax.rsqrt(_group_mean(od * od, g64) + EPS) * gains[2:3] * _silu(zd_ref[...])
    out = _dot(ya, w_ref[0:GW, :])
    out = out + _dot(yb, w_ref[GW:2 * GW, :])
    out = out + _dot(yc, w_ref[2 * GW:3 * GW, :])
    out = out + _dot(yd, w_ref[3 * GW:4 * GW, :])
    xo_ref[...] = x_ref[...] + mod_ref[0, 0][2:3] * out


def _post_call(X, modtab, oa, ob, hcf, hcb, odf, odb, U, gains, g64, w_out, B, T, lam_init):
    nt = T // TM
    d = X.shape[1]
    row = lambda b, i: (b * nt + i, 0)
    const = lambda b, i: (0, 0)
    gspec = pl.BlockSpec((TM, GW), row)
    return pl.pallas_call(
        functools.partial(_post_kernel, lam_init=lam_init), grid=(B, nt),
        in_specs=[pl.BlockSpec((TM, d), row),
                  pl.BlockSpec((1, 1, 8, d), lambda b, i: (b, jnp.minimum(i, 1), 0, 0)),
                  gspec, gspec, gspec, gspec, gspec, gspec,
                  pl.BlockSpec((TM, GW), lambda b, i: (b * nt + i, U_CO // GW)),
                  pl.BlockSpec((TM, GW), lambda b, i: (b * nt + i, U_DZ // GW)),
                  pl.BlockSpec((8, GW), const), pl.BlockSpec((GW, GW), const),
                  pl.BlockSpec(w_out.shape, const)],
        out_specs=pl.BlockSpec((TM, d), row),
        out_shape=jax.ShapeDtypeStruct(X.shape, F32),
        compiler_params=_cparams(("arbitrary", "arbitrary")), name="out_proj",
    )(X, modtab, oa, ob, hcf, hcb, odf, odb, U, U, gains, g64, w_out)


FF_CHUNK = 256


def _ffn_kernel(x_ref, mod_ref, gain_ref, wg_ref, wu_ref, wd_ref, xo_ref):
    x = x_ref[...]
    mod = mod_ref[0, 0]
    h = _norm_mod(x, gain_ref[...], mod[3:4], mod[4:5]).astype(BF16)
    acc = jnp.zeros(x.shape, F32)
    for f in range(0, wg_ref.shape[1], FF_CHUNK):
        g = lax.dot_general(h, wg_ref[:, f:f + FF_CHUNK], NN, preferred_element_type=F32)
        u = lax.dot_general(h, wu_ref[:, f:f + FF_CHUNK], NN, preferred_element_type=F32)
        acc = acc + _dot(_silu(g) * u, wd_ref[f:f + FF_CHUNK, :])
    xo_ref[...] = x + mod[5:6] * acc


def _ffn_call(X, modtab, gain, wg, wu, wd, nb, tiles_per_b, kind_of_tile):
    d = X.shape[1]
    row = lambda b, i: (b * tiles_per_b + i, 0)
    const = lambda b, i: (0, 0)
    return pl.pallas_call(
        _ffn_kernel, grid=(nb, tiles_per_b),
        in_specs=[pl.BlockSpec((TM, d), row),
                  pl.BlockSpec((1, 1, 8, d), lambda b, i: (b, kind_of_tile(i), 0, 0)),
                  pl.BlockSpec((1, d), const),
                  pl.BlockSpec(wg.shape, const), pl.BlockSpec(wu.shape, const), pl.BlockSpec(wd.shape, const)],
        out_specs=pl.BlockSpec((TM, d), row),
        out_shape=jax.ShapeDtypeStruct(X.shape, F32),
        compiler_params=_cparams(("arbitrary", "arbitrary")), name="ffn_dense",
    )(X, modtab, gain, wg, wu, wd)


def _router_kernel(x_ref, mod_ref, gain_ref, r_ref, h_ref, gates_ref, *, n_exp):
    x = x_ref[...]
    mod = mod_ref[0, 0]
    h = _norm_mod(x, gain_ref[...], mod[3:4], mod[4:5])
    h_ref[...] = h.astype(BF16)
    logits = lax.dot_general(h, r_ref[...], NN, precision=lax.Precision.HIGHEST, preferred_element_type=F32)
    lane = lax.broadcasted_iota(jnp.int32, logits.shape, 1)
    neg = -jnp.inf
    lg = jnp.where(lane < n_exp, logits, neg)
    v1 = jnp.max(lg, axis=1, keepdims=True)
    i1 = jnp.min(jnp.where(lg == v1, lane, LANES), axis=1, keepdims=True)
    lg2 = jnp.where(lane == i1, neg, lg)
    v2 = jnp.max(lg2, axis=1, keepdims=True)
    i2 = jnp.min(jnp.where(lg2 == v2, lane, LANES), axis=1, keepdims=True)
    e2 = jnp.exp(v2 - v1)
    w1 = 1.0 / (1.0 + e2)
    w2 = e2 / (1.0 + e2)
    gates_ref[...] = jnp.where(lane == i1, w1, 0.0) + jnp.where(lane == i2, w2, 0.0)


def _router_call(X, modtab, gain, router128, nb, tiles_per_b, n_exp):
    d = X.shape[1]
    row = lambda b, i: (b * tiles_per_b + i, 0)
    const = lambda b, i: (0, 0)
    return pl.pallas_call(
        functools.partial(_router_kernel, n_exp=n_exp), grid=(nb, tiles_per_b),
        in_specs=[pl.BlockSpec((TM, d), row),
                  pl.BlockSpec((1, 1, 8, d), lambda b, i: (b, 1, 0, 0)),
                  pl.BlockSpec((1, d), const),
                  pl.BlockSpec(router128.shape, const)],
        out_specs=[pl.BlockSpec((TM, d), row), pl.BlockSpec((TM, LANES), row)],
        out_shape=[jax.ShapeDtypeStruct(X.shape, BF16), jax.ShapeDtypeStruct((X.shape[0], LANES), F32)],
        compiler_params=_cparams(("arbitrary", "arbitrary")), name="moe_router",
    )(X, modtab, gain, router128)


MOE_TM = 1024
MOE_C0 = 640
MOE_CG = 128
MOE_CH = 256
MOE_TF = 896


def _moe_kernel(x_ref, mod_ref, h_ref, gates_ref, tri_ref, wg_ref, wu_ref, wd_ref, xo_ref,
                acc_sc, rk_sc, rkt_sc, xg0_sc, y0_sc, xg_sc, y_sc, nck_sm):
    e = pl.program_id(2)
    f = pl.program_id(3)
    ne = pl.num_programs(2)
    nf = pl.num_programs(3)
    tm = h_ref.shape[0]
    c0 = xg0_sc.shape[0]
    sizes = tuple(range(MOE_CG, c0 + 1, MOE_CG))

    def for_block0_size(fn):
        for idx, rows in enumerate(sizes):
            pl.when(nck_sm[0] == idx)(functools.partial(fn, rows))

    def onehot(base, rows, rrow):
        r = lax.broadcasted_iota(jnp.int32, (rows, 1), 0).astype(F32) + base
        return jnp.where(r == rrow, 1.0, 0.0).astype(BF16)

    def onehot_t(base, rows, rcol):
        c = lax.broadcasted_iota(jnp.int32, (1, rows), 1).astype(F32) + base
        return jnp.where(rcol == c, 1.0, 0.0).astype(BF16)

    def ffn(xk):
        g = lax.dot_general(xk, wg_ref[0], NN, preferred_element_type=F32)
        u = lax.dot_general(xk, wu_ref[0], NN, preferred_element_type=F32)
        return _dot(_silu(g) * u, wd_ref[0])

    def rest_base(k):
        return (c0 + k * MOE_CH).astype(F32)

    @pl.when((e == 0) & (f == 0))
    def _():
        acc_sc[...] = jnp.zeros_like(acc_sc)
        sel = gates_ref[...] > 0.0
        rank = lax.dot_general(tri_ref[...], jnp.where(sel, 1.0, 0.0).astype(BF16), NN,
                               preferred_element_type=F32)
        rk = jnp.where(sel, rank, -1.0)
        rk_sc[...] = rk
        rkt_sc[...] = rk.T

    @pl.when(f == 0)
    def _():
        rrow = rkt_sc[pl.ds(e, 1), :]
        cnt = jnp.sum(jnp.where(rrow >= 0.0, 1.0, 0.0))
        icnt = cnt.astype(jnp.int32)
        nck_sm[0] = jnp.clip((icnt + (MOE_CG - 1)) // MOE_CG, 1, len(sizes)) - 1
        nck_sm[1] = (jnp.maximum(icnt - c0, 0) + (MOE_CH - 1)) // MOE_CH
        h = h_ref[...]

        def gather0(rows):
            xg0_sc[0:rows] = lax.dot_general(onehot(0.0, rows, rrow), h, NN,
                                             preferred_element_type=F32).astype(BF16)
            y0_sc[0:rows] = jnp.zeros((rows, y0_sc.shape[1]), F32)

        for_block0_size(gather0)

        def gather(k, carry):
            xg_sc[k] = lax.dot_general(onehot(rest_base(k), MOE_CH, rrow), h, NN,
                                       preferred_element_type=F32).astype(BF16)
            y_sc[k] = jnp.zeros((MOE_CH, y_sc.shape[2]), F32)
            return carry

        lax.fori_loop(0, nck_sm[1], gather, 0)

    def expert0(rows):
        y0_sc[0:rows] += ffn(xg0_sc[0:rows])

    for_block0_size(expert0)

    def expert(k, carry):
        y_sc[k] += ffn(xg_sc[k])
        return carry

    lax.fori_loop(0, nck_sm[1], expert, 0)

    @pl.when(f == nf - 1)
    def _():
        lane = lax.broadcasted_iota(jnp.int32, (tm, LANES), 1)
        rcol = jnp.sum(jnp.where(lane == e, rk_sc[...], 0.0), axis=1, keepdims=True)
        gcol = jnp.sum(jnp.where(lane == e, gates_ref[...], 0.0), axis=1, keepdims=True)

        def scatter0(rows):
            acc_sc[...] += gcol * lax.dot_general(onehot_t(0.0, rows, rcol), y0_sc[0:rows].astype(BF16), NN,
                                                  preferred_element_type=F32)

        for_block0_size(scatter0)

        def scatter(k, carry):
            acc_sc[...] += gcol * lax.dot_general(onehot_t(rest_base(k), MOE_CH, rcol), y_sc[k].astype(BF16), NN,
                                                  preferred_element_type=F32)
            return carry

        lax.fori_loop(0, nck_sm[1], scatter, 0)

    @pl.when((e == ne - 1) & (f == nf - 1))
    def _():
        xo_ref[...] = x_ref[...] + mod_ref[0, 0][5:6] * acc_sc[...]


def _moe_call(X, modtab, hb, gates, wg, wu, wd, nb, rows_per_b):
    d = X.shape[1]
    n_exp, _, ff = wg.shape
    tm = min(MOE_TM, rows_per_b)
    tpb = rows_per_b // tm
    c0 = min(MOE_C0, tm)
    nrest = max(-(-(tm - c0) // MOE_CH), 1)
    tri = jnp.asarray(np.tril(np.ones((tm, tm), np.float32), -1), BF16)
    row = lambda b, i, e, f: (b * tpb + i, 0)
    return pl.pallas_call(
        _moe_kernel, grid=(nb, tpb, n_exp, ff // MOE_TF),
        in_specs=[pl.BlockSpec((tm, d), row),
                  pl.BlockSpec((1, 1, 8, d), lambda b, i, e, f: (b, 1, 0, 0)),
                  pl.BlockSpec((tm, d), row),
                  pl.BlockSpec((tm, LANES), row),
                  pl.BlockSpec((tm, tm), lambda b, i, e, f: (0, 0)),
                  pl.BlockSpec((1, d, MOE_TF), lambda b, i, e, f: (e, 0, f)),
                  pl.BlockSpec((1, d, MOE_TF), lambda b, i, e, f: (e, 0, f)),
                  pl.BlockSpec((1, MOE_TF, d), lambda b, i, e, f: (e, f, 0))],
        out_specs=pl.BlockSpec((tm, d), row),
        out_shape=jax.ShapeDtypeStruct(X.shape, F32),
        scratch_shapes=[pltpu.VMEM((tm, d), F32), pltpu.VMEM((tm, LANES), F32), pltpu.VMEM((LANES, tm), F32),
                        pltpu.VMEM((c0, d), BF16), pltpu.VMEM((c0, d), F32),
                        pltpu.VMEM((nrest, MOE_CH, d), BF16), pltpu.VMEM((nrest, MOE_CH, d), F32),
                        pltpu.SMEM((2,), jnp.int32)],
        compiler_params=_cparams(("arbitrary",) * 4), name="moe_experts",
    )(X, modtab, hb, gates, tri, wg, wu, wd)


def _group_mean_matrix(group):
    idx = np.arange(GW)
    return jnp.asarray((idx[:, None] // group == idx[None, :] // group).astype(np.float32) / group, BF16)


def _rope_tables(n_lat, n_ctx, dim, reps):
    t = jnp.arange(n_lat)
    pos = jnp.stack([(t // GRID_W).astype(F32), (t % GRID_W).astype(F32)], axis=1)
    n_freq = dim // 4
    inv_freq = ROPE_THETA ** (-jnp.arange(n_freq, dtype=F32) / n_freq)
    ang = pos[:, :, None] * inv_freq
    cos = jnp.cos(ang)
    sin = jnp.sin(ang)
    cos = jnp.stack([cos, cos], axis=2).reshape(n_lat, dim)
    sin = jnp.stack([-sin, sin], axis=2).reshape(n_lat, dim)
    cos = jnp.concatenate([jnp.ones((n_ctx, dim), F32), cos], axis=0)
    sin = jnp.concatenate([jnp.zeros((n_ctx, dim), F32), sin], axis=0)
    return jnp.tile(cos, (1, reps)), jnp.tile(sin, (1, reps))


def _expand_matrix(col0):
    m = np.zeros((LANES, GW), np.float32)
    for h in range(4):
        m[col0 + h, h * HD:(h + 1) * HD] = 1.0
    return m


_SPLITS = (256, 256, 256, 256, 128, 128, 256, 256, 256, 256, 16, 768, 256, 16)


def _reorder_w_in(w):
    offs = np.concatenate([[0], np.cumsum(_SPLITS)])
    seg = [w[:, offs[i]:offs[i + 1]] for i in range(len(_SPLITS))]
    aq, ak, av, bq, bk, bv, cq, ck, cv, co, cg, dqkv, dz, dg = seg
    pad = jnp.zeros((w.shape[0], LANES - 32), w.dtype)
    return jnp.concatenate([aq, ak, av, bq, bk, bv, dqkv, cq, ck, cv, co, dz, cg, dg, pad], axis=1).astype(BF16)


def kernel(x, c, ctx, c_ctx, w_mod, b_mod, norm1, norm2, w_in, w_out, diff_qk_gain, diff_lambda, diff_subln,
           gqa_qk_gain, mlstm_gate_bias, mlstm_norm, gdn_conv, gdn_a_log, gdn_dt_bias, gdn_norm,
           ffn_w_gate, ffn_w_up, ffn_w_down, moe_router, moe_w_gate, moe_w_up, moe_w_down):
    B, S, D = x.shape
    CT = ctx.shape[1]
    T = CT + S
    depth = w_in.shape[0]
    assert CT % TM == 0 and S % TM == 0 and S % GRID_W == 0 and B + 1 <= 8

    g32 = _group_mean_matrix(HD // 2)
    g64 = _group_mean_matrix(HD)
    cosa, sina = _rope_tables(S, CT, HD // 2, 8)
    cosb, sinb = _rope_tables(S, CT, HD, 4)
    e_i = jnp.asarray(np.stack([_expand_matrix(0), _expand_matrix(8)]), F32)
    e_f = jnp.asarray(np.stack([_expand_matrix(4), _expand_matrix(12)]), F32)
    e_d = jnp.asarray(np.stack([_expand_matrix(16 + 4 * j) for j in range(4)]), F32)

    cvec = jnp.zeros((8, D), F32).at[:B].set(c).at[B].set(c_ctx)
    mod_all = _mod_call(cvec, w_mod, b_mod).reshape(depth, 8, 6, D)

    X = jnp.concatenate([ctx, x], axis=1).reshape(B * T, D)
    nb_rows = T
    for li in range(depth):
        with_ctx = li < depth - 1
        lam_init = 0.8 - 0.6 * math.exp(-0.3 * li)
        m = mod_all[li]
        modtab = jnp.stack([jnp.broadcast_to(m[B], (B, 6, D)), m[:B]], axis=1)
        modtab = jnp.pad(modtab, ((0, 0), (0, 0), (0, 2), (0, 0)))

        qkg = jnp.zeros((8, GW), F32)
        qkg = qkg.at[0].set(jnp.tile(diff_qk_gain[li, 0], 8)).at[1].set(jnp.tile(diff_qk_gain[li, 1], 8))
        qkg = qkg.at[2].set(jnp.tile(gqa_qk_gain[li, 0], 4)).at[3].set(jnp.tile(gqa_qk_gain[li, 1], 4))
        qta, ka, vta, qtb, kb, vtb, U = _kin_call(
            X, modtab, norm1[li][None, :], _reorder_w_in(w_in[li]), g32, g64, qkg, cosa, sina, cosb, sinb, B, T)

        oa = _attn_call(qta, ka, vta, diff_lambda[li], B, T, CT, True, lam_init, "attn_diff")
        ob = _attn_call(qtb, kb, vtb, diff_lambda[li], B, T, CT, False, lam_init, "attn_gqa")

        bias128 = jnp.zeros((1, LANES), F32).at[0, :16].set(mlstm_gate_bias[li].reshape(-1))
        hcf = _mlstm_call(U, bias128, e_i, e_f, B, T, CT, False)
        hcb = _mlstm_call(U, bias128, e_i, e_f, B, T, CT, True)

        alog_e = jnp.repeat(gdn_a_log[li], HD, axis=1)
        dtb_e = jnp.repeat(gdn_dt_bias[li], HD, axis=1)
        qd, kd, vd, bf, gf, bb, gb = _gdn_prep_call(U, gdn_conv[li], g64, e_d, alog_e, dtb_e, B, T, CT)
        odf = _gdn_call(qd, kd, vd, bf, gf, B, T, CT, False)
        odb = _gdn_call(qd, kd, vd, bb, gb, B, T, CT, True)

        gains = jnp.zeros((8, GW), F32)
        gains = gains.at[0].set(jnp.tile(diff_subln[li], 4)).at[1].set(mlstm_norm[li])
        gains = gains.at[2].set(jnp.tile(gdn_norm[li], 4))
        X = _post_call(X, modtab, oa, ob, hcf, hcb, odf, odb, U, gains, g64, w_out[li].astype(BF16), B, T, lam_init)

        if with_ctx:
            tiles_per_b, kind = T // TM, (lambda i: jnp.minimum(i, 1))
        else:
            X = X.reshape(B, T, D)[:, CT:].reshape(B * S, D)
            nb_rows = S
            tiles_per_b, kind = S // TM, (lambda i: 1)
        j = li // 2
        if li % 2 == 0:
            X = _ffn_call(X, modtab, norm2[li][None, :], ffn_w_gate[j].astype(BF16), ffn_w_up[j].astype(BF16),
                          ffn_w_down[j].astype(BF16), B, tiles_per_b, kind)
        else:
            assert not with_ctx
            n_exp = moe_router.shape[2]
            router128 = jnp.pad(moe_router[j], ((0, 0), (0, LANES - n_exp)))
            hb, gates = _router_call(X, modtab, norm2[li][None, :], router128, B, tiles_per_b, n_exp)
            X = _moe_call(X, modtab, hb, gates, moe_w_gate[j].astype(BF16), moe_w_up[j].astype(BF16),
                          moe_w_down[j].astype(BF16), B, nb_rows)
    return X.reshape(B, nb_rows, D)[:, nb_rows - S:]
```

```python
import functools
import math

import numpy as np
import jax
import jax.numpy as jnp
from jax import lax
from jax.experimental import pallas as pl
from jax.experimental.pallas import tpu as pltpu

F32 = jnp.float32
BF16 = jnp.bfloat16
EPS = 1e-6
ROPE_THETA = 10000.0
GRID_W = 64
LOG2E = 1.4426950408889634

TM = 256
CH = 64
HD = 64
GW = 256
LANES = 128
KV_BLKS = 1
V_ONES = 16
VMEM_LIMIT = 56 * 1024 * 1024

NN = (((1,), (0,)), ((), ()))
NT = (((1,), (1,)), ((), ()))
TN = (((0,), (0,)), ((), ()))


def _dot(a, b, dims=NN):
    return lax.dot_general(a.astype(BF16), b.astype(BF16), dims, preferred_element_type=F32)


def _split(a):
    hi = a.astype(BF16)
    lo = (a - hi.astype(F32)).astype(BF16)
    return hi, lo


def _dot_xl(a, b_exact, dims=NN):
    hi, lo = _split(a)
    bb = b_exact.astype(BF16)
    return (lax.dot_general(hi, bb, dims, preferred_element_type=F32)
            + lax.dot_general(lo, bb, dims, preferred_element_type=F32))


def _dot_xr(a_exact, b, dims=NN):
    hi, lo = _split(b)
    aa = a_exact.astype(BF16)
    return (lax.dot_general(aa, hi, dims, preferred_element_type=F32)
            + lax.dot_general(aa, lo, dims, preferred_element_type=F32))


def _dot_x3(a, b, dims=NN):
    ah, al = _split(a)
    bh, bl = _split(b)
    d = functools.partial(lax.dot_general, dimension_numbers=dims, preferred_element_type=F32)
    return d(ah, bh) + d(ah, bl) + d(al, bh)


def _sigmoid(x):
    return 1.0 / (1.0 + jnp.exp(-x))


def _silu(x):
    return x * _sigmoid(x)


def _softplus(x):
    return jnp.maximum(x, 0.0) + jnp.log(1.0 + jnp.exp(-jnp.abs(x)))


def _cparams(sem, vmem=VMEM_LIMIT):
    return pltpu.CompilerParams(dimension_semantics=sem, vmem_limit_bytes=vmem)


def _mod_kernel(c_ref, w_ref, b_ref, o_ref):
    cv = c_ref[...]
    s = _silu(cv)
    o_ref[0] = lax.dot_general(s, w_ref[0], NN, precision=lax.Precision.HIGHEST,
                               preferred_element_type=F32) + b_ref[0]


def _mod_call(cvec, w_mod, b_mod):
    depth, d, n6 = w_mod.shape
    rows = cvec.shape[0]
    tn = 1536
    return pl.pallas_call(
        _mod_kernel,
        grid=(depth, n6 // tn),
        in_specs=[pl.BlockSpec((rows, d), lambda l, j: (0, 0)),
                  pl.BlockSpec((1, d, tn), lambda l, j: (l, 0, j)),
                  pl.BlockSpec((1, 1, tn), lambda l, j: (l, 0, j))],
        out_specs=pl.BlockSpec((1, rows, tn), lambda l, j: (l, 0, j)),
        out_shape=jax.ShapeDtypeStruct((depth, rows, n6), F32),
        compiler_params=_cparams(("arbitrary", "arbitrary")),
        name="adaln_mod",
    )(cvec, w_mod, b_mod.reshape(depth, 1, n6))


def _norm_mod(x, gain, shift, scale):
    ms = jnp.mean(x * x, axis=-1, keepdims=True)
    return (x * lax.rsqrt(ms + EPS) * gain) * (1.0 + scale) + shift


def _group_mean(v, g_ref):
    return _dot_xl(v, g_ref[...])


def _rope(xn, cos, sin, half):
    w = xn.shape[1]
    lane = lax.broadcasted_iota(jnp.int32, xn.shape, 1)
    first = (lane % (2 * half)) < half
    xs = jnp.where(first, pltpu.roll(xn, w - half, 1), pltpu.roll(xn, half, 1))
    return xn * cos + xs * sin


A_W = 3 * GW
B_W = GW + 2 * 128
AB_W = A_W + B_W
U_W = 2176
U_DQKV, U_CQ, U_CK, U_CV, U_CO, U_DZ, U_G = 0, 768, 1024, 1280, 1536, 1792, 2048


def _kin_kernel(x_ref, mod_ref, gain_ref, w_ref, g32_ref, g64_ref, qkg_ref,
                cosa_ref, sina_ref, cosb_ref, sinb_ref,
                qta_ref, ka_ref, vta_ref, qtb_ref, kb_ref, vtb_ref, u_ref):
    x = x_ref[...]
    mod = mod_ref[0, 0]
    h = _norm_mod(x, gain_ref[...], mod[0:1], mod[1:2]).astype(BF16)

    ya = lax.dot_general(h, w_ref[:, 0:A_W], NN, preferred_element_type=F32)
    qkg = qkg_ref[...]
    ca, sa = cosa_ref[...], sina_ref[...]

    def prep(v, g_ref, gain, cos, sin, half):
        vn = v * lax.rsqrt(_group_mean(v * v, g_ref) + EPS) * gain
        return _rope(vn, cos, sin, half)

    qa = prep(ya[:, 0:GW], g32_ref, qkg[0:1], ca, sa, 8) * ((HD // 2) ** -0.5 * LOG2E)
    ka = prep(ya[:, GW:2 * GW], g32_ref, qkg[1:2], ca, sa, 8)
    qta_ref[0, 0] = qa.T.astype(BF16)
    ka_ref[...] = ka.astype(BF16)
    vta_ref[0, 0] = ya[:, 2 * GW:3 * GW].T.astype(BF16)

    yb = lax.dot_general(h, w_ref[:, A_W:AB_W], NN, preferred_element_type=F32)
    cb, sb = cosb_ref[...], sinb_ref[...]
    qb = prep(yb[:, 0:GW], g64_ref, qkg[2:3], cb, sb, 16) * (HD ** -0.5 * LOG2E)
    kb = prep(yb[:, GW:GW + 128], g64_ref.at[0:128, 0:128], qkg[3:4, 0:128], cb[:, 0:128], sb[:, 0:128], 16)
    qtb_ref[0, 0] = qb.T.astype(BF16)
    kb_ref[...] = kb.astype(BF16)
    vtb_ref[0, 0] = yb[:, GW + 128:GW + 256].T.astype(BF16)

    u_ref[...] = lax.dot_general(h, w_ref[:, AB_W:AB_W + U_W], NN, preferred_element_type=F32)


def _kin_call(X, modtab, gain, w_r, g32, g64, qkg, cosa, sina, cosb, sinb, B, T):
    nt = T // TM
    d = X.shape[1]
    wcols = w_r.shape[1]
    row = lambda b, i: (b * nt + i, 0)
    const = lambda b, i: (0, 0)
    pos = lambda b, i: (i, 0)
    tmap = lambda b, i: (b, i, 0, 0)
    in_specs = [
        pl.BlockSpec((TM, d), row),
        pl.BlockSpec((1, 1, 8, d), lambda b, i: (b, jnp.minimum(i, 1), 0, 0)),
        pl.BlockSpec((1, d), const),
        pl.BlockSpec((d, wcols), const),
        pl.BlockSpec((GW, GW), const),
        pl.BlockSpec((GW, GW), const),
        pl.BlockSpec((8, GW), const),
        pl.BlockSpec((TM, GW), pos), pl.BlockSpec((TM, GW), pos),
        pl.BlockSpec((TM, GW), pos), pl.BlockSpec((TM, GW), pos),
    ]
    out_specs = [
        pl.BlockSpec((1, 1, GW, TM), tmap), pl.BlockSpec((TM, GW), row), pl.BlockSpec((1, 1, GW, TM), tmap),
        pl.BlockSpec((1, 1, GW, TM), tmap), pl.BlockSpec((TM, 128), row), pl.BlockSpec((1, 1, 128, TM), tmap),
        pl.BlockSpec((TM, U_W), row),
    ]
    out_shape = [
        jax.ShapeDtypeStruct((B, nt, GW, TM), BF16), jax.ShapeDtypeStruct((B * T, GW), BF16),
        jax.ShapeDtypeStruct((B, nt, GW, TM), BF16),
        jax.ShapeDtypeStruct((B, nt, GW, TM), BF16), jax.ShapeDtypeStruct((B * T, 128), BF16),
        jax.ShapeDtypeStruct((B, nt, 128, TM), BF16),
        jax.ShapeDtypeStruct((B * T, U_W), F32),
    ]
    return pl.pallas_call(
        _kin_kernel, grid=(B, nt), in_specs=in_specs, out_specs=out_specs, out_shape=out_shape,
        compiler_params=_cparams(("arbitrary", "arbitrary")), name="in_proj",
    )(X, modtab, gain, w_r, g32, g64, qkg, cosa, sina, cosb, sinb)


def _attn_kernel(qt_ref, k_ref, vt_ref, lam_ref, o_ref, sa_sc, sb_sc, *,
                 diff, lam_init, n_ctx_tiles, n_ctx_blk, n_all_blk):
    i = pl.program_id(1)
    n_lat_pairs = jnp.where(i < n_ctx_tiles, 0, (n_all_blk - n_ctx_blk) // (2 * KV_BLKS))
    qt = qt_ref[0, 0]
    kd = k_ref.shape[1]
    rowi = lax.broadcasted_iota(jnp.int32, (kd, TM), 0)

    def run_streams(qms, vlos):
        def scores(j0, nblk):
            off = pl.multiple_of(j0 * TM, TM)
            kb = k_ref[pl.ds(off, nblk * TM), :]
            return [lax.dot_general(kb, qm, NN, preferred_element_type=F32) for qm in qms]

        def update(ss, j0, carry, nblk):
            stats = []
            for (m, _), s in zip(carry, ss):
                m_new = jnp.maximum(m, jnp.max(s, axis=0, keepdims=True))
                stats.append((m_new, jnp.exp2(m - m_new), jnp.exp2((s - m_new).astype(BF16))))
            ones = jnp.ones((V_ONES, nblk * TM), BF16)
            new = []
            for (_, acc), (m_new, alpha, p), vlo in zip(carry, stats, vlos):
                vb = jnp.concatenate([vt_ref[0, j0 + t, vlo:vlo + HD, :] for t in range(nblk)], axis=1)
                acc = alpha * acc + lax.dot_general(jnp.concatenate([vb, ones], axis=0), p, NN,
                                                    preferred_element_type=F32)
                new.append((m_new, acc))
            return tuple(new)

        def scores_into(buf, j0):
            for si, s in enumerate(scores(j0, KV_BLKS)):
                buf[si] = s

        def pair(it, carry):
            ja = n_ctx_blk + it * (2 * KV_BLKS)
            jb = ja + KV_BLKS
            jn = jnp.minimum(jb + KV_BLKS, n_all_blk - KV_BLKS)
            scores_into(sb_sc, jb)
            carry = update([sa_sc[si] for si in range(len(qms))], ja, carry, KV_BLKS)
            scores_into(sa_sc, jn)
            return update([sb_sc[si] for si in range(len(qms))], jb, carry, KV_BLKS)

        m0 = jnp.full((1, TM), -jnp.inf, F32)
        a0 = jnp.zeros((HD + V_ONES, TM), F32)
        scores_into(sa_sc, n_ctx_blk)
        res = lax.fori_loop(0, n_ctx_blk, lambda j, c: update(scores(j, 1), j, c, 1),
                            tuple((m0, a0) for _ in qms))
        res = lax.fori_loop(0, n_lat_pairs, pair, res)
        return [acc[0:HD] * (1.0 / acc[HD:HD + 1]) for (_, acc) in res]

    outs = []
    if diff:
        lv = lam_ref[...]
        lam = (jnp.exp(jnp.sum(lv[0:1] * lv[1:2], axis=1, keepdims=True))
               - jnp.exp(jnp.sum(lv[2:3] * lv[3:4], axis=1, keepdims=True)) + lam_init)
        for hp in range(2):
            qms, vlos = [], []
            for h in (2 * hp, 2 * hp + 1):
                for j in range(2):
                    lo = (2 * h + j) * (HD // 2)
                    qms.append(jnp.where((rowi >= lo) & (rowi < lo + HD // 2), qt, jnp.zeros_like(qt)))
                    vlos.append(h * HD)
            r = run_streams(qms, vlos)
            outs += [r[0] - lam * r[1], r[2] - lam * r[3]]
    else:
        qms, vlos = [], []
        for h in range(4):
            kvh = h // 2
            qh = qt[h * HD:(h + 1) * HD, :]
            z = jnp.zeros_like(qh)
            qms.append(jnp.concatenate([qh, z] if kvh == 0 else [z, qh], axis=0))
            vlos.append(kvh * HD)
        outs = run_streams(qms, vlos)
    o_ref[...] = jnp.concatenate(outs, axis=0).T


def _attn_call(qt, k, vt, lam, B, T, CT, diff, lam_init, name):
    nt = T // TM
    kd = k.shape[1]
    vr = vt.shape[2]
    assert (nt - CT // TM) % (2 * KV_BLKS) == 0
    kern = functools.partial(_attn_kernel, diff=diff, lam_init=lam_init, n_ctx_tiles=CT // TM,
                             n_ctx_blk=CT // TM, n_all_blk=nt)
    score_buf = pltpu.VMEM((4, KV_BLKS * TM, TM), F32)
    return pl.pallas_call(
        kern, grid=(B, nt),
        in_specs=[pl.BlockSpec((1, 1, GW, TM), lambda b, i: (b, i, 0, 0)),
                  pl.BlockSpec((T, kd), lambda b, i: (b, 0)),
                  pl.BlockSpec((1, nt, vr, TM), lambda b, i: (b, 0, 0, 0)),
                  pl.BlockSpec(lam.shape, lambda b, i: (0, 0))],
        out_specs=pl.BlockSpec((TM, GW), lambda b, i: (b * nt + i, 0)),
        out_shape=jax.ShapeDtypeStruct((B * T, GW), F32),
        scratch_shapes=[score_buf, score_buf],
        compiler_params=_cparams(("arbitrary", "arbitrary")), name=name,
    )(qt, k, vt, lam)


def _chunk_masks(reverse):
    row = lax.broadcasted_iota(jnp.int32, (CH, GW), 0)
    lane = lax.broadcasted_iota(jnp.int32, (CH, GW), 1)
    sidx = lane % CH
    eye = row == sidx
    causal = (sidx >= row) if reverse else (sidx <= row)
    strict = (sidx > row) if reverse else (sidx < row)
    r2 = lax.broadcasted_iota(jnp.int32, (GW, GW), 0)
    c2 = lax.broadcasted_iota(jnp.int32, (GW, GW), 1)
    blk = (r2 // CH) == (c2 // CH)
    t1 = lax.broadcasted_iota(jnp.int32, (CH, CH), 0)
    t2 = lax.broadcasted_iota(jnp.int32, (CH, CH), 1)
    tri = ((t2 >= t1) if reverse else (t2 <= t1)).astype(F32)
    return eye, causal, strict, blk, tri


def _bd(y, blk):
    y16 = y.astype(BF16)
    return jnp.where(blk, jnp.concatenate([y16, y16, y16, y16], axis=0), jnp.zeros((), BF16))


def _dot_x3_bd(x, y, blk):
    xh, xl = _split(x)
    yh, yl = _split(y)
    bh, bl = _bd(yh, blk), _bd(yl, blk)
    d = functools.partial(lax.dot_general, dimension_numbers=NN, preferred_element_type=F32)
    n = x.shape[0]
    both = d(jnp.concatenate([xh, xl], axis=0), bh)
    return both[0:n] + both[n:2 * n] + d(xh, bl)


def _row_form(col, eye):
    return _dot_xr(jnp.ones((CH, CH), F32), jnp.where(eye, col, 0.0))


def _cummax(a, reverse):
    n = a.shape[0]
    row = lax.broadcasted_iota(jnp.int32, a.shape, 0)
    k = 1
    while k < n:
        if reverse:
            sh, valid = pltpu.roll(a, n - k, 0), row < n - k
        else:
            sh, valid = pltpu.roll(a, k, 0), row >= k
        a = jnp.maximum(a, jnp.where(valid, sh, -jnp.inf))
        k *= 2
    return a


def _chunk_order(p, nc_ctx, nch, reverse):
    if not reverse:
        return p
    return jnp.where(p < nc_ctx, nc_ctx - 1 - p, nch + nc_ctx - 1 - p)


def _mlstm_kernel(q_ref, k_ref, v_ref, g_ref, bias_ref, ei_ref, ef_ref, h_ref, c_sc, n_sc, m_sc, *, reverse):
    @pl.when(pl.program_id(0) == 0)
    def _():
        c_sc[...] = jnp.zeros_like(c_sc)
        n_sc[...] = jnp.zeros_like(n_sc)
        m_sc[...] = jnp.zeros_like(m_sc)

    eye, causal, _, blk, tri = _chunk_masks(reverse)
    last = 0 if reverse else CH - 1
    R = range(q_ref.shape[0])
    q4 = [q_ref[b] for b in R]
    k4 = [k_ref[b] * (HD ** -0.5) for b in R]
    v4 = [v_ref[b] for b in R]
    gt = [g_ref[b] + bias_ref[...] for b in R]
    i_e = [_dot_xl(gt[b], ei_ref[0]) for b in R]
    f_e = [_dot_xl(gt[b], ef_ref[0]) for b in R]
    logf = [jnp.minimum(f_e[b], 0.0) - jnp.log(1.0 + jnp.exp(-jnp.abs(f_e[b]))) for b in R]
    b_e = [_dot_xr(tri, logf[b]) for b in R]
    m_prev = [m_sc[b, 0:1, :] for b in R]
    a_e = [i_e[b] - b_e[b] for b in R]
    rowa = [_row_form(a_e[b], eye) for b in R]
    inter = [b_e[b] + m_prev[b] for b in R]
    m_t = [jnp.maximum(inter[b], b_e[b] + _cummax(a_e[b], reverse)) for b in R]
    w_inter = [jnp.exp(inter[b] - m_t[b]) for b in R]
    dexp = [jnp.exp(jnp.where(causal, b_e[b] - m_t[b] + rowa[b], -jnp.inf)) for b in R]
    smat = [_dot(q4[b], _bd(k4[b], blk), NT) * dexp[b] for b in R]
    cbd = [c_sc[b] for b in R]
    nbd = [n_sc[b] for b in R]
    qc = [_dot(q4[b], cbd[b]) for b in R]
    qn = [_dot(q4[b], nbd[b]) for b in R]
    sv = [_dot(smat[b], _bd(v4[b], blk)) for b in R]
    ssum = [_dot_xl(smat[b], blk.astype(F32)) for b in R]
    b_last = [b_e[b][last:last + 1, :] for b in R]
    g = [b_last[b] - b_e[b] + i_e[b] for b in R]
    m_new = [jnp.maximum(b_last[b] + m_prev[b], jnp.max(g[b], axis=0, keepdims=True)) for b in R]
    ks = [k4[b] * jnp.exp(g[b] - m_new[b]) for b in R]
    cup = [_dot(ks[b], v4[b], TN) for b in R]
    nup = [_dot(ks[b], jnp.ones((CH, GW), F32), TN) for b in R]
    for b in R:
        num = w_inter[b] * qc[b] + sv[b]
        den = w_inter[b] * qn[b] + ssum[b]
        h_ref[b] = num / jnp.maximum(jnp.abs(den), jnp.exp(-m_t[b]))
        a_prev = jnp.exp(b_last[b] + m_prev[b] - m_new[b])
        c_sc[b] = a_prev * cbd[b] + jnp.where(blk, cup[b], 0.0)
        n_sc[b] = a_prev * nbd[b] + jnp.where(blk, nup[b], 0.0)
        m_sc[b] = jnp.broadcast_to(m_new[b], (8, GW))


def _mlstm_call(U, bias128, e_i, e_f, B, T, CT, reverse):
    nch, ncc = T // CH, CT // CH
    U3 = U.reshape(B, T, U.shape[1])
    cmap = lambda col: (lambda p: (0, _chunk_order(p, ncc, nch, reverse), col))
    d = 1 if reverse else 0
    out = pl.pallas_call(
        functools.partial(_mlstm_kernel, reverse=reverse), grid=(nch,),
        in_specs=[pl.BlockSpec((B, CH, GW), cmap(U_CQ // GW)),
                  pl.BlockSpec((B, CH, GW), cmap(U_CK // GW)),
                  pl.BlockSpec((B, CH, GW), cmap(U_CV // GW)),
                  pl.BlockSpec((B, CH, LANES), cmap(U_G // LANES)),
                  pl.BlockSpec((1, LANES), lambda p: (0, 0)),
                  pl.BlockSpec((1, LANES, GW), lambda p: (d, 0, 0)),
                  pl.BlockSpec((1, LANES, GW), lambda p: (d, 0, 0))],
        out_specs=pl.BlockSpec((B, CH, GW), cmap(0)),
        out_shape=jax.ShapeDtypeStruct((B, T, GW), F32),
        scratch_shapes=[pltpu.VMEM((B, GW, GW), F32), pltpu.VMEM((B, GW, GW), F32), pltpu.VMEM((B, 8, GW), F32)],
        compiler_params=_cparams(("arbitrary",)),
        name="mlstm_bwd" if reverse else "mlstm_fwd",
    )(U3, U3, U3, U3, bias128, e_i, e_f)
    return out.reshape(B * T, GW)


def _gdn_prep_kernel(prev_ref, cur_ref, next_ref, g_ref, conv_ref, g64_ref, e_ref, alog_ref, dtb_ref,
                     q_ref, k_ref, v_ref, bf_ref, gf_ref, bb_ref, gb_ref, *, n_ctx_tiles, nt):
    i = pl.program_id(1)
    at_start = (i == 0) | (i == n_ctx_tiles)
    at_end = (i == n_ctx_tiles - 1) | (i == nt - 1)
    cur = cur_ref[...]
    head = jnp.where(at_start, 0.0, prev_ref[...])
    tail = jnp.where(at_end, 0.0, next_ref[...])
    xx = jnp.concatenate([head, cur, tail], axis=0)
    cw = conv_ref[...]
    kw = cw.shape[0]
    acc = jnp.zeros_like(cur)
    for j in range(kw):
        s0 = 8 + j - kw // 2
        acc = acc + xx[s0:s0 + TM, :] * cw[j:j + 1, :]
    t = _silu(acc)
    q, k, v = t[:, 0:GW], t[:, GW:2 * GW], t[:, 2 * GW:3 * GW]
    g64 = g64_ref[...] * float(HD)
    q_ref[...] = q * lax.rsqrt(_dot_xl(q * q, g64) + EPS) * (HD ** -0.5)
    k_ref[...] = k * lax.rsqrt(_dot_xl(k * k, g64) + EPS)
    v_ref[...] = v
    gt = g_ref[...]
    for d, (b_out, g_out) in enumerate(((bf_ref, gf_ref), (bb_ref, gb_ref))):
        b_out[...] = _sigmoid(_dot_xl(gt, e_ref[2 * d]))
        a_e = _dot_xl(gt, e_ref[2 * d + 1])
        g_out[...] = -jnp.exp(alog_ref[d:d + 1, :]) * _softplus(a_e + dtb_ref[d:d + 1, :])


def _gdn_prep_call(U, conv_w, g64, e_d, alog_e, dtb_e, B, T, CT):
    nt = T // TM
    w3 = 3 * GW
    r8 = TM // 8
    row = lambda b, i: (b * nt + i, 0)
    const2 = lambda b, i: (0, 0)
    o = jax.ShapeDtypeStruct((B * T, GW), F32)
    return pl.pallas_call(
        functools.partial(_gdn_prep_kernel, n_ctx_tiles=CT // TM, nt=nt), grid=(B, nt),
        in_specs=[pl.BlockSpec((8, w3), lambda b, i: (jnp.maximum((b * nt + i) * r8 - 1, 0), 0)),
                  pl.BlockSpec((TM, w3), row),
                  pl.BlockSpec((8, w3), lambda b, i: (jnp.minimum((b * nt + i + 1) * r8, B * nt * r8 - 1), 0)),
                  pl.BlockSpec((TM, LANES), lambda b, i: (b * nt + i, U_G // LANES)),
                  pl.BlockSpec(conv_w.shape, const2),
                  pl.BlockSpec((GW, GW), const2),
                  pl.BlockSpec(e_d.shape, lambda b, i: (0, 0, 0)),
                  pl.BlockSpec((2, GW), const2), pl.BlockSpec((2, GW), const2)],
        out_specs=[pl.BlockSpec((TM, GW), row)] * 7,
        out_shape=[o] * 7,
        compiler_params=_cparams(("arbitrary", "arbitrary")), name="gdn_prep",
    )(U, U, U, U, conv_w, g64, e_d, alog_e, dtb_e)


def _gdn_kernel(q_ref, k_ref, v_ref, beta_ref, g_ref, o_ref, s_sc, *, reverse):
    @pl.when(pl.program_id(0) == 0)
    def _():
        s_sc[...] = jnp.zeros_like(s_sc)

    eye, causal, strict, blk, tri = _chunk_masks(reverse)
    last = 0 if reverse else CH - 1
    R = range(q_ref.shape[0])
    q4, k4, v4, beta = ([r[b] for b in R] for r in (q_ref, k_ref, v_ref, beta_ref))
    gam = [_dot_xr(tri, g_ref[b]) for b in R]
    rowg = [_row_form(gam[b], eye) for b in R]
    decay = [jnp.exp(jnp.where(causal, gam[b] - rowg[b], -jnp.inf)) for b in R]
    kbeta = [k4[b] * beta[b] for b in R]
    kbd = [_bd(k4[b], blk) for b in R]
    a4 = [jnp.where(strict, _dot(kbeta[b], kbd[b], NT) * decay[b], 0.0) for b in R]
    attn = [_dot(q4[b], kbd[b], NT) * decay[b] for b in R]
    tinv = [jnp.where(eye, 1.0, 0.0) - a4[b] for b in R]
    apow = a4
    for _ in range(5):
        apow = [_dot_x3_bd(apow[b], apow[b], blk) for b in R]
        tinv = [tinv[b] + _dot_x3_bd(tinv[b], apow[b], blk) for b in R]
    u4 = [_dot_x3_bd(tinv[b], v4[b] * beta[b], blk) for b in R]
    w4 = [_dot_x3_bd(tinv[b], kbeta[b] * jnp.exp(gam[b]), blk) for b in R]
    sbd = [s_sc[b] for b in R]
    v_new = [u4[b] - _dot(w4[b], sbd[b]) for b in R]
    o_inter = [_dot(q4[b] * jnp.exp(gam[b]), sbd[b]) for b in R]
    o_intra = [_dot(attn[b], _bd(v_new[b], blk)) for b in R]
    upd = [_dot(k4[b] * jnp.exp(gam[b][last:last + 1, :] - gam[b]), v_new[b], TN) for b in R]
    for b in R:
        o_ref[b] = o_inter[b] + o_intra[b]
        s_sc[b] = sbd[b] * jnp.exp(gam[b][last:last + 1, :]) + jnp.where(blk, upd[b], 0.0)


def _gdn_call(q, k, v, beta, g, B, T, CT, reverse):
    nch, ncc = T // CH, CT // CH
    spec = pl.BlockSpec((B, CH, GW), lambda p: (0, _chunk_order(p, ncc, nch, reverse), 0))
    r3 = lambda a: a.reshape(B, T, GW)
    out = pl.pallas_call(
        functools.partial(_gdn_kernel, reverse=reverse), grid=(nch,),
        in_specs=[spec] * 5, out_specs=spec,
        out_shape=jax.ShapeDtypeStruct((B, T, GW), F32),
        scratch_shapes=[pltpu.VMEM((B, GW, GW), F32)],
        compiler_params=_cparams(("arbitrary",)),
        name="gdn_bwd" if reverse else "gdn_fwd",
    )(r3(q), r3(k), r3(v), r3(beta), r3(g))
    return out.reshape(B * T, GW)


def _post_kernel(x_ref, mod_ref, oa_ref, ob_ref, hcf_ref, hcb_ref, odf_ref, odb_ref, oc_ref, zd_ref,
                 gains_ref, g64_ref, w_ref, xo_ref, *, lam_init):
    gains = gains_ref[...]
    g64 = g64_ref

    oa = oa_ref[...]
    ya = oa * lax.rsqrt(_group_mean(oa * oa, g64) + EPS) * gains[0:1] * (1.0 - lam_init)
    yb = ob_ref[...]
    hc = hcf_ref[...] + hcb_ref[...]
    dc = hc - _group_mean(hc, g64)
    yc = _sigmoid(oc_ref[...]) * (dc * lax.rsqrt(_group_mean(dc * dc, g64) + EPS) * gains[1:2])
    od = odf_ref[...] + odb_ref[...]
    yd = od * lax.rsqrt(_group_mean(od * od, g64) + EPS) * gains[2:3] * _silu(zd_ref[...])
    out = _dot(ya, w_ref[0:GW, :])
    out = out + _dot(yb, w_ref[GW:2 * GW, :])
    out = out + _dot(yc, w_ref[2 * GW:3 * GW, :])
    out = out + _dot(yd, w_ref[3 * GW:4 * GW, :])
    xo_ref[...] = x_ref[...] + mod_ref[0, 0][2:3] * out


def _post_call(X, modtab, oa, ob, hcf, hcb, odf, odb, U, gains, g64, w_out, B, T, lam_init):
    nt = T // TM
    d = X.shape[1]
    row = lambda b, i: (b * nt + i, 0)
    const = lambda b, i: (0, 0)
    gspec = pl.BlockSpec((TM, GW), row)
    return pl.pallas_call(
        functools.partial(_post_kernel, lam_init=lam_init), grid=(B, nt),
        in_specs=[pl.BlockSpec((TM, d), row),
                  pl.BlockSpec((1, 1, 8, d), lambda b, i: (b, jnp.minimum(i, 1), 0, 0)),
                  gspec, gspec, gspec, gspec, gspec, gspec,
                  pl.BlockSpec((TM, GW), lambda b, i: (b * nt + i, U_CO // GW)),
                  pl.BlockSpec((TM, GW), lambda b, i: (b * nt + i, U_DZ // GW)),
                  pl.BlockSpec((8, GW), const), pl.BlockSpec((GW, GW), const),
                  pl.BlockSpec(w_out.shape, const)],
        out_specs=pl.BlockSpec((TM, d), row),
        out_shape=jax.ShapeDtypeStruct(X.shape, F32),
        compiler_params=_cparams(("arbitrary", "arbitrary")), name="out_proj",
    )(X, modtab, oa, ob, hcf, hcb, odf, odb, U, U, gains, g64, w_out)


FF_CHUNK = 256


def _ffn_kernel(x_ref, mod0_ref, mod1_ref, gain_ref, wg_ref, wu_ref, wd_ref, xo_ref):
    x = x_ref[...]
    mods = (mod0_ref[0, 0], mod1_ref[0, 0])
    h = jnp.concatenate([_norm_mod(x[t * TM:(t + 1) * TM], gain_ref[...], m[3:4], m[4:5])
                         for t, m in enumerate(mods)], axis=0).astype(BF16)
    acc = jnp.zeros(x.shape, F32)
    for f in range(0, wg_ref.shape[1], FF_CHUNK):
        g = lax.dot_general(h, wg_ref[:, f:f + FF_CHUNK], NN, preferred_element_type=F32)
        u = lax.dot_general(h, wu_ref[:, f:f + FF_CHUNK], NN, preferred_element_type=F32)
        acc = acc + _dot(_silu(g) * u, wd_ref[f:f + FF_CHUNK, :])
    for t, m in enumerate(mods):
        xo_ref[t * TM:(t + 1) * TM, :] = x[t * TM:(t + 1) * TM] + m[5:6] * acc[t * TM:(t + 1) * TM]


def _ffn_call(X, modtab, gain, wg, wu, wd, nb, tiles_per_b, kind_of_tile):
    d = X.shape[1]
    n_tiles = nb * tiles_per_b
    assert n_tiles % 2 == 0
    const = lambda p: (0, 0)
    mod_of = lambda half: (lambda p: ((2 * p + half) // tiles_per_b, kind_of_tile((2 * p + half) % tiles_per_b), 0, 0))
    return pl.pallas_call(
        _ffn_kernel, grid=(n_tiles // 2,),
        in_specs=[pl.BlockSpec((2 * TM, d), lambda p: (p, 0)),
                  pl.BlockSpec((1, 1, 8, d), mod_of(0)),
                  pl.BlockSpec((1, 1, 8, d), mod_of(1)),
                  pl.BlockSpec((1, d), const),
                  pl.BlockSpec(wg.shape, const), pl.BlockSpec(wu.shape, const), pl.BlockSpec(wd.shape, const)],
        out_specs=pl.BlockSpec((2 * TM, d), lambda p: (p, 0)),
        out_shape=jax.ShapeDtypeStruct(X.shape, F32),
        compiler_params=_cparams(("arbitrary",)), name="ffn_dense",
    )(X, modtab, modtab, gain, wg, wu, wd)


def _router_kernel(x_ref, mod_ref, gain_ref, r_ref, h_ref, gates_ref, *, n_exp):
    x = x_ref[...]
    mod = mod_ref[0, 0]
    h = _norm_mod(x, gain_ref[...], mod[3:4], mod[4:5])
    h_ref[...] = h.astype(BF16)
    logits = lax.dot_general(h, r_ref[...], NN, precision=lax.Precision.HIGHEST, preferred_element_type=F32)
    lane = lax.broadcasted_iota(jnp.int32, logits.shape, 1)
    neg = -jnp.inf
    lg = jnp.where(lane < n_exp, logits, neg)
    v1 = jnp.max(lg, axis=1, keepdims=True)
    i1 = jnp.min(jnp.where(lg == v1, lane, LANES), axis=1, keepdims=True)
    lg2 = jnp.where(lane == i1, neg, lg)
    v2 = jnp.max(lg2, axis=1, keepdims=True)
    i2 = jnp.min(jnp.where(lg2 == v2, lane, LANES), axis=1, keepdims=True)
    e2 = jnp.exp(v2 - v1)
    w1 = 1.0 / (1.0 + e2)
    w2 = e2 / (1.0 + e2)
    gates_ref[...] = jnp.where(lane == i1, w1, 0.0) + jnp.where(lane == i2, w2, 0.0)


def _router_call(X, modtab, gain, router128, nb, tiles_per_b, n_exp):
    d = X.shape[1]
    row = lambda b, i: (b * tiles_per_b + i, 0)
    const = lambda b, i: (0, 0)
    return pl.pallas_call(
        functools.partial(_router_kernel, n_exp=n_exp), grid=(nb, tiles_per_b),
        in_specs=[pl.BlockSpec((TM, d), row),
                  pl.BlockSpec((1, 1, 8, d), lambda b, i: (b, 1, 0, 0)),
                  pl.BlockSpec((1, d), const),
                  pl.BlockSpec(router128.shape, const)],
        out_specs=[pl.BlockSpec((TM, d), row), pl.BlockSpec((TM, LANES), row)],
        out_shape=[jax.ShapeDtypeStruct(X.shape, BF16), jax.ShapeDtypeStruct((X.shape[0], LANES), F32)],
        compiler_params=_cparams(("arbitrary", "arbitrary")), name="moe_router",
    )(X, modtab, gain, router128)


MOE_TM = 1024
MOE_C0 = 640
MOE_CG = 128
MOE_CH = 256
MOE_TF = 896


def _moe_kernel(x_ref, mod_ref, h_ref, gates_ref, tri_ref, wg_ref, wu_ref, wd_ref, xo_ref,
                acc_sc, rk_sc, rkt_sc, xg0_sc, y0_sc, xg_sc, y_sc, nck_sm):
    e = pl.program_id(2)
    f = pl.program_id(3)
    ne = pl.num_programs(2)
    nf = pl.num_programs(3)
    tm = h_ref.shape[0]
    c0 = xg0_sc.shape[0]
    sizes = tuple(range(MOE_CG, c0 + 1, MOE_CG))

    def for_block0_size(fn):
        for idx, rows in enumerate(sizes):
            pl.when(nck_sm[0] == idx)(functools.partial(fn, rows))

    def onehot(base, rows, rrow):
        r = lax.broadcasted_iota(jnp.int32, (rows, 1), 0).astype(F32) + base
        return jnp.where(r == rrow, 1.0, 0.0).astype(BF16)

    def onehot_t(base, rows, rcol):
        c = lax.broadcasted_iota(jnp.int32, (1, rows), 1).astype(F32) + base
        return jnp.where(rcol == c, 1.0, 0.0).astype(BF16)

    def ffn(xk):
        g = lax.dot_general(xk, wg_ref[0], NN, preferred_element_type=F32)
        u = lax.dot_general(xk, wu_ref[0], NN, preferred_element_type=F32)
        return _dot(_silu(g) * u, wd_ref[0])

    def rest_base(k):
        return (c0 + k * MOE_CH).astype(F32)

    @pl.when((e == 0) & (f == 0))
    def _():
        acc_sc[...] = jnp.zeros_like(acc_sc)
        sel = gates_ref[...] > 0.0
        rank = lax.dot_general(tri_ref[...], jnp.where(sel, 1.0, 0.0).astype(BF16), NN,
                               preferred_element_type=F32)
        rk = jnp.where(sel, rank, -1.0)
        rk_sc[...] = rk
        rkt_sc[...] = rk.T

    @pl.when(f == 0)
    def _():
        rrow = rkt_sc[pl.ds(e, 1), :]
        cnt = jnp.sum(jnp.where(rrow >= 0.0, 1.0, 0.0))
        icnt = cnt.astype(jnp.int32)
        nck_sm[0] = jnp.clip((icnt + (MOE_CG - 1)) // MOE_CG, 1, len(sizes)) - 1
        nck_sm[1] = (jnp.maximum(icnt - c0, 0) + (MOE_CH - 1)) // MOE_CH
        h = h_ref[...]

        def gather0(rows):
            xg0_sc[0:rows] = lax.dot_general(onehot(0.0, rows, rrow), h, NN,
                                             preferred_element_type=F32).astype(BF16)
            y0_sc[0:rows] = jnp.zeros((rows, y0_sc.shape[1]), F32)

        for_block0_size(gather0)

        def gather(k, carry):
            xg_sc[k] = lax.dot_general(onehot(rest_base(k), MOE_CH, rrow), h, NN,
                                       preferred_element_type=F32).astype(BF16)
            y_sc[k] = jnp.zeros((MOE_CH, y_sc.shape[2]), F32)
            return carry

        lax.fori_loop(0, nck_sm[1], gather, 0)

    def expert0(rows):
        y0_sc[0:rows] += ffn(xg0_sc[0:rows])

    for_block0_size(expert0)

    def expert(k, carry):
        y_sc[k] += ffn(xg_sc[k])
        return carry

    lax.fori_loop(0, nck_sm[1], expert, 0)

    @pl.when(f == nf - 1)
    def _():
        lane = lax.broadcasted_iota(jnp.int32, (tm, LANES), 1)
        rcol = jnp.sum(jnp.where(lane == e, rk_sc[...], 0.0), axis=1, keepdims=True)
        gcol = jnp.sum(jnp.where(lane == e, gates_ref[...], 0.0), axis=1, keepdims=True)

        def scatter0(rows):
            acc_sc[...] += gcol * lax.dot_general(onehot_t(0.0, rows, rcol), y0_sc[0:rows].astype(BF16), NN,
                                                  preferred_element_type=F32)

        for_block0_size(scatter0)

        def scatter(k, carry):
            acc_sc[...] += gcol * lax.dot_general(onehot_t(rest_base(k), MOE_CH, rcol), y_sc[k].astype(BF16), NN,
                                                  preferred_element_type=F32)
            return carry

        lax.fori_loop(0, nck_sm[1], scatter, 0)

    @pl.when((e == ne - 1) & (f == nf - 1))
    def _():
        xo_ref[...] = x_ref[...] + mod_ref[0, 0][5:6] * acc_sc[...]


def _moe_call(X, modtab, hb, gates, wg, wu, wd, nb, rows_per_b):
    d = X.shape[1]
    n_exp, _, ff = wg.shape
    tm = min(MOE_TM, rows_per_b)
    tpb = rows_per_b // tm
    c0 = min(MOE_C0, tm)
    nrest = max(-(-(tm - c0) // MOE_CH), 1)
    tri = jnp.asarray(np.tril(np.ones((tm, tm), np.float32), -1), BF16)
    row = lambda b, i, e, f: (b * tpb + i, 0)
    return pl.pallas_call(
        _moe_kernel, grid=(nb, tpb, n_exp, ff // MOE_TF),
        in_specs=[pl.BlockSpec((tm, d), row),
                  pl.BlockSpec((1, 1, 8, d), lambda b, i, e, f: (b, 1, 0, 0)),
                  pl.BlockSpec((tm, d), row),
                  pl.BlockSpec((tm, LANES), row),
                  pl.BlockSpec((tm, tm), lambda b, i, e, f: (0, 0)),
                  pl.BlockSpec((1, d, MOE_TF), lambda b, i, e, f: (e, 0, f)),
                  pl.BlockSpec((1, d, MOE_TF), lambda b, i, e, f: (e, 0, f)),
                  pl.BlockSpec((1, MOE_TF, d), lambda b, i, e, f: (e, f, 0))],
        out_specs=pl.BlockSpec((tm, d), row),
        out_shape=jax.ShapeDtypeStruct(X.shape, F32),
        scratch_shapes=[pltpu.VMEM((tm, d), F32), pltpu.VMEM((tm, LANES), F32), pltpu.VMEM((LANES, tm), F32),
                        pltpu.VMEM((c0, d), BF16), pltpu.VMEM((c0, d), F32),
                        pltpu.VMEM((nrest, MOE_CH, d), BF16), pltpu.VMEM((nrest, MOE_CH, d), F32),
                        pltpu.SMEM((2,), jnp.int32)],
        compiler_params=_cparams(("arbitrary",) * 4), name="moe_experts",
    )(X, modtab, hb, gates, tri, wg, wu, wd)


def _group_mean_matrix(group):
    idx = np.arange(GW)
    return jnp.asarray((idx[:, None] // group == idx[None, :] // group).astype(np.float32) / group, BF16)


def _rope_tables(n_lat, n_ctx, dim, reps):
    t = jnp.arange(n_lat)
    pos = jnp.stack([(t // GRID_W).astype(F32), (t % GRID_W).astype(F32)], axis=1)
    n_freq = dim // 4
    inv_freq = ROPE_THETA ** (-jnp.arange(n_freq, dtype=F32) / n_freq)
    ang = pos[:, :, None] * inv_freq
    cos = jnp.cos(ang)
    sin = jnp.sin(ang)
    cos = jnp.stack([cos, cos], axis=2).reshape(n_lat, dim)
    sin = jnp.stack([-sin, sin], axis=2).reshape(n_lat, dim)
    cos = jnp.concatenate([jnp.ones((n_ctx, dim), F32), cos], axis=0)
    sin = jnp.concatenate([jnp.zeros((n_ctx, dim), F32), sin], axis=0)
    return jnp.tile(cos, (1, reps)), jnp.tile(sin, (1, reps))


def _expand_matrix(col0):
    m = np.zeros((LANES, GW), np.float32)
    for h in range(4):
        m[col0 + h, h * HD:(h + 1) * HD] = 1.0
    return m


_SPLITS = (256, 256, 256, 256, 128, 128, 256, 256, 256, 256, 16, 768, 256, 16)


def _reorder_w_in(w):
    offs = np.concatenate([[0], np.cumsum(_SPLITS)])
    seg = [w[:, offs[i]:offs[i + 1]] for i in range(len(_SPLITS))]
    aq, ak, av, bq, bk, bv, cq, ck, cv, co, cg, dqkv, dz, dg = seg
    pad = jnp.zeros((w.shape[0], LANES - 32), w.dtype)
    return jnp.concatenate([aq, ak, av, bq, bk, bv, dqkv, cq, ck, cv, co, dz, cg, dg, pad], axis=1).astype(BF16)


def kernel(x, c, ctx, c_ctx, w_mod, b_mod, norm1, norm2, w_in, w_out, diff_qk_gain, diff_lambda, diff_subln,
           gqa_qk_gain, mlstm_gate_bias, mlstm_norm, gdn_conv, gdn_a_log, gdn_dt_bias, gdn_norm,
           ffn_w_gate, ffn_w_up, ffn_w_down, moe_router, moe_w_gate, moe_w_up, moe_w_down):
    B, S, D = x.shape
    CT = ctx.shape[1]
    T = CT + S
    depth = w_in.shape[0]
    assert CT % TM == 0 and S % TM == 0 and S % GRID_W == 0 and B + 1 <= 8

    g32 = _group_mean_matrix(HD // 2)
    g64 = _group_mean_matrix(HD)
    cosa, sina = _rope_tables(S, CT, HD // 2, 8)
    cosb, sinb = _rope_tables(S, CT, HD, 4)
    e_i = jnp.asarray(np.stack([_expand_matrix(0), _expand_matrix(8)]), F32)
    e_f = jnp.asarray(np.stack([_expand_matrix(4), _expand_matrix(12)]), F32)
    e_d = jnp.asarray(np.stack([_expand_matrix(16 + 4 * j) for j in range(4)]), F32)

    cvec = jnp.zeros((8, D), F32).at[:B].set(c).at[B].set(c_ctx)
    mod_all = _mod_call(cvec, w_mod, b_mod).reshape(depth, 8, 6, D)

    X = jnp.concatenate([ctx, x], axis=1).reshape(B * T, D)
    nb_rows = T
    for li in range(depth):
        with_ctx = li < depth - 1
        lam_init = 0.8 - 0.6 * math.exp(-0.3 * li)
        m = mod_all[li]
        modtab = jnp.stack([jnp.broadcast_to(m[B], (B, 6, D)), m[:B]], axis=1)
        modtab = jnp.pad(modtab, ((0, 0), (0, 0), (0, 2), (0, 0)))

        qkg = jnp.zeros((8, GW), F32)
        qkg = qkg.at[0].set(jnp.tile(diff_qk_gain[li, 0], 8)).at[1].set(jnp.tile(diff_qk_gain[li, 1], 8))
        qkg = qkg.at[2].set(jnp.tile(gqa_qk_gain[li, 0], 4)).at[3].set(jnp.tile(gqa_qk_gain[li, 1], 4))
        qta, ka, vta, qtb, kb, vtb, U = _kin_call(
            X, modtab, norm1[li][None, :], _reorder_w_in(w_in[li]), g32, g64, qkg, cosa, sina, cosb, sinb, B, T)

        oa = _attn_call(qta, ka, vta, diff_lambda[li], B, T, CT, True, lam_init, "attn_diff")
        ob = _attn_call(qtb, kb, vtb, diff_lambda[li], B, T, CT, False, lam_init, "attn_gqa")

        bias128 = jnp.zeros((1, LANES), F32).at[0, :16].set(mlstm_gate_bias[li].reshape(-1))
        hcf = _mlstm_call(U, bias128, e_i, e_f, B, T, CT, False)
        hcb = _mlstm_call(U, bias128, e_i, e_f, B, T, CT, True)

        alog_e = jnp.repeat(gdn_a_log[li], HD, axis=1)
        dtb_e = jnp.repeat(gdn_dt_bias[li], HD, axis=1)
        qd, kd, vd, bf, gf, bb, gb = _gdn_prep_call(U, gdn_conv[li], g64, e_d, alog_e, dtb_e, B, T, CT)
        odf = _gdn_call(qd, kd, vd, bf, gf, B, T, CT, False)
        odb = _gdn_call(qd, kd, vd, bb, gb, B, T, CT, True)

        gains = jnp.zeros((8, GW), F32)
        gains = gains.at[0].set(jnp.tile(diff_subln[li], 4)).at[1].set(mlstm_norm[li])
        gains = gains.at[2].set(jnp.tile(gdn_norm[li], 4))
        X = _post_call(X, modtab, oa, ob, hcf, hcb, odf, odb, U, gains, g64, w_out[li].astype(BF16), B, T, lam_init)

        if with_ctx:
            tiles_per_b, kind = T // TM, (lambda i: jnp.minimum(i, 1))
        else:
            X = X.reshape(B, T, D)[:, CT:].reshape(B * S, D)
            nb_rows = S
            tiles_per_b, kind = S // TM, (lambda i: 1)
        j = li // 2
        if li % 2 == 0:
            X = _ffn_call(X, modtab, norm2[li][None, :], ffn_w_gate[j].astype(BF16), ffn_w_up[j].astype(BF16),
                          ffn_w_down[j].astype(BF16), B, tiles_per_b, kind)
        else:
            assert not with_ctx
            n_exp = moe_router.shape[2]
            router128 = jnp.pad(moe_router[j], ((0, 0), (0, LANES - n_exp)))
            hb, gates = _router_call(X, modtab, norm2[li][None, :], router128, B, tiles_per_b, n_exp)
            X = _moe_call(X, modtab, hb, gates, moe_w_gate[j].astype(BF16), moe_w_up[j].astype(BF16),
                          moe_w_down[j].astype(BF16), B, nb_rows)
    return X.reshape(B, nb_rows, D)[:, nb_rows - S:]
```

```python
import functools
import math

import numpy as np
import jax
import jax.numpy as jnp
from jax import lax
from jax.experimental import pallas as pl
from jax.experimental.pallas import tpu as pltpu

F32 = jnp.float32
BF16 = jnp.bfloat16
EPS = 1e-6
ROPE_THETA = 10000.0
GRID_W = 64
LOG2E = 1.4426950408889634

TM = 256
CH = 64
HD = 64
GW = 256
LANES = 128
KV_BLKS_DIFF = 2
KV_BLKS_GQA = 1
V_ONES = 16
VMEM_LIMIT = 56 * 1024 * 1024

NN = (((1,), (0,)), ((), ()))
NT = (((1,), (1,)), ((), ()))
TN = (((0,), (0,)), ((), ()))


def _dot(a, b, dims=NN):
    return lax.dot_general(a.astype(BF16), b.astype(BF16), dims, preferred_element_type=F32)


def _split(a):
    hi = a.astype(BF16)
    lo = (a - hi.astype(F32)).astype(BF16)
    return hi, lo


def _dot_xl(a, b_exact, dims=NN):
    hi, lo = _split(a)
    bb = b_exact.astype(BF16)
    return (lax.dot_general(hi, bb, dims, preferred_element_type=F32)
            + lax.dot_general(lo, bb, dims, preferred_element_type=F32))


def _dot_xr(a_exact, b, dims=NN):
    hi, lo = _split(b)
    aa = a_exact.astype(BF16)
    return (lax.dot_general(aa, hi, dims, preferred_element_type=F32)
            + lax.dot_general(aa, lo, dims, preferred_element_type=F32))


def _dot_x3(a, b, dims=NN):
    ah, al = _split(a)
    bh, bl = _split(b)
    d = functools.partial(lax.dot_general, dimension_numbers=dims, preferred_element_type=F32)
    return d(ah, bh) + d(ah, bl) + d(al, bh)


def _sigmoid(x):
    return 1.0 / (1.0 + jnp.exp(-x))


def _silu(x):
    return x * _sigmoid(x)


def _softplus(x):
    return jnp.maximum(x, 0.0) + jnp.log(1.0 + jnp.exp(-jnp.abs(x)))


def _cparams(sem, vmem=VMEM_LIMIT):
    return pltpu.CompilerParams(dimension_semantics=sem, vmem_limit_bytes=vmem)


def _mod_kernel(c_ref, w_ref, b_ref, o_ref):
    cv = c_ref[...]
    s = _silu(cv)
    o_ref[0] = lax.dot_general(s, w_ref[0], NN, precision=lax.Precision.HIGHEST,
                               preferred_element_type=F32) + b_ref[0]


def _mod_call(cvec, w_mod, b_mod):
    depth, d, n6 = w_mod.shape
    rows = cvec.shape[0]
    tn = 1536
    return pl.pallas_call(
        _mod_kernel,
        grid=(depth, n6 // tn),
        in_specs=[pl.BlockSpec((rows, d), lambda l, j: (0, 0)),
                  pl.BlockSpec((1, d, tn), lambda l, j: (l, 0, j)),
                  pl.BlockSpec((1, 1, tn), lambda l, j: (l, 0, j))],
        out_specs=pl.BlockSpec((1, rows, tn), lambda l, j: (l, 0, j)),
        out_shape=jax.ShapeDtypeStruct((depth, rows, n6), F32),
        compiler_params=_cparams(("arbitrary", "arbitrary")),
        name="adaln_mod",
    )(cvec, w_mod, b_mod.reshape(depth, 1, n6))


def _norm_mod(x, gain, shift, scale):
    ms = jnp.mean(x * x, axis=-1, keepdims=True)
    return (x * lax.rsqrt(ms + EPS) * gain) * (1.0 + scale) + shift


def _group_mean(v, g_ref):
    return _dot_xl(v, g_ref[...])


def _rope(xn, cos, sin, half):
    w = xn.shape[1]
    lane = lax.broadcasted_iota(jnp.int32, xn.shape, 1)
    first = (lane % (2 * half)) < half
    xs = jnp.where(first, pltpu.roll(xn, w - half, 1), pltpu.roll(xn, half, 1))
    return xn * cos + xs * sin


A_W = 3 * GW
B_W = GW + 2 * 128
AB_W = A_W + B_W
U_W = 2176
U_DQKV, U_CQ, U_CK, U_CV, U_CO, U_DZ, U_G = 0, 768, 1024, 1280, 1536, 1792, 2048


def _kin_kernel(x_ref, mod_ref, gain_ref, w_ref, g32_ref, g64_ref, qkg_ref,
                cosa_ref, sina_ref, cosb_ref, sinb_ref,
                qta_ref, ka_ref, vta_ref, qtb_ref, kb_ref, vtb_ref, u_ref):
    x = x_ref[...]
    mod = mod_ref[0, 0]
    h = _norm_mod(x, gain_ref[...], mod[0:1], mod[1:2]).astype(BF16)

    ya = lax.dot_general(h, w_ref[:, 0:A_W], NN, preferred_element_type=F32)
    qkg = qkg_ref[...]
    ca, sa = cosa_ref[...], sina_ref[...]

    def prep(v, g_ref, gain, cos, sin, half):
        vn = v * lax.rsqrt(_group_mean(v * v, g_ref) + EPS) * gain
        return _rope(vn, cos, sin, half)

    qa = prep(ya[:, 0:GW], g32_ref, qkg[0:1], ca, sa, 8) * ((HD // 2) ** -0.5 * LOG2E)
    ka = prep(ya[:, GW:2 * GW], g32_ref, qkg[1:2], ca, sa, 8)
    qta_ref[0, 0] = qa.T.astype(BF16)
    ka_ref[...] = ka.astype(BF16)
    vta_ref[0, 0] = ya[:, 2 * GW:3 * GW].T.astype(BF16)

    yb = lax.dot_general(h, w_ref[:, A_W:AB_W], NN, preferred_element_type=F32)
    cb, sb = cosb_ref[...], sinb_ref[...]
    qb = prep(yb[:, 0:GW], g64_ref, qkg[2:3], cb, sb, 16) * (HD ** -0.5 * LOG2E)
    kb = prep(yb[:, GW:GW + 128], g64_ref.at[0:128, 0:128], qkg[3:4, 0:128], cb[:, 0:128], sb[:, 0:128], 16)
    qtb_ref[0, 0] = qb.T.astype(BF16)
    kb_ref[...] = kb.astype(BF16)
    vtb_ref[0, 0] = yb[:, GW + 128:GW + 256].T.astype(BF16)

    u_ref[...] = lax.dot_general(h, w_ref[:, AB_W:AB_W + U_W], NN, preferred_element_type=F32)


def _kin_call(X, modtab, gain, w_r, g32, g64, qkg, cosa, sina, cosb, sinb, B, T):
    nt = T // TM
    d = X.shape[1]
    wcols = w_r.shape[1]
    row = lambda b, i: (b * nt + i, 0)
    const = lambda b, i: (0, 0)
    pos = lambda b, i: (i, 0)
    tmap = lambda b, i: (b, i, 0, 0)
    in_specs = [
        pl.BlockSpec((TM, d), row),
        pl.BlockSpec((1, 1, 8, d), lambda b, i: (b, jnp.minimum(i, 1), 0, 0)),
        pl.BlockSpec((1, d), const),
        pl.BlockSpec((d, wcols), const),
        pl.BlockSpec((GW, GW), const),
        pl.BlockSpec((GW, GW), const),
        pl.BlockSpec((8, GW), const),
        pl.BlockSpec((TM, GW), pos), pl.BlockSpec((TM, GW), pos),
        pl.BlockSpec((TM, GW), pos), pl.BlockSpec((TM, GW), pos),
    ]
    out_specs = [
        pl.BlockSpec((1, 1, GW, TM), tmap), pl.BlockSpec((TM, GW), row), pl.BlockSpec((1, 1, GW, TM), tmap),
        pl.BlockSpec((1, 1, GW, TM), tmap), pl.BlockSpec((TM, 128), row), pl.BlockSpec((1, 1, 128, TM), tmap),
        pl.BlockSpec((TM, U_W), row),
    ]
    out_shape = [
        jax.ShapeDtypeStruct((B, nt, GW, TM), BF16), jax.ShapeDtypeStruct((B * T, GW), BF16),
        jax.ShapeDtypeStruct((B, nt, GW, TM), BF16),
        jax.ShapeDtypeStruct((B, nt, GW, TM), BF16), jax.ShapeDtypeStruct((B * T, 128), BF16),
        jax.ShapeDtypeStruct((B, nt, 128, TM), BF16),
        jax.ShapeDtypeStruct((B * T, U_W), F32),
    ]
    return pl.pallas_call(
        _kin_kernel, grid=(B, nt), in_specs=in_specs, out_specs=out_specs, out_shape=out_shape,
        compiler_params=_cparams(("arbitrary", "arbitrary")), name="in_proj",
    )(X, modtab, gain, w_r, g32, g64, qkg, cosa, sina, cosb, sinb)


def _attn_kernel(qt_ref, k_ref, vt_ref, lam_ref, o_ref, sa_sc, sb_sc, *,
                 diff, lam_init, n_ctx_tiles, n_ctx_blk, n_all_blk):
    KV_BLKS = sa_sc.shape[1] // TM
    i = pl.program_id(1)
    n_lat_pairs = jnp.where(i < n_ctx_tiles, 0, (n_all_blk - n_ctx_blk) // (2 * KV_BLKS))
    qt = qt_ref[0, 0]
    kd = k_ref.shape[1]
    rowi = lax.broadcasted_iota(jnp.int32, (kd, TM), 0)

    def run_streams(qms, vlos):
        def scores(j0, nblk):
            off = pl.multiple_of(j0 * TM, TM)
            kb = k_ref[pl.ds(off, nblk * TM), :]
            return [lax.dot_general(kb, qm, NN, preferred_element_type=F32) for qm in qms]

        def update(ss, j0, carry, nblk):
            stats = []
            for (m, _), s in zip(carry, ss):
                m_new = jnp.maximum(m, jnp.max(s, axis=0, keepdims=True))
                stats.append((m_new, jnp.exp2(m - m_new), jnp.exp2((s - m_new).astype(BF16))))
            ones = jnp.ones((V_ONES, nblk * TM), BF16)
            new = []
            for (_, acc), (m_new, alpha, p), vlo in zip(carry, stats, vlos):
                vb = jnp.concatenate([vt_ref[0, j0 + t, vlo:vlo + HD, :] for t in range(nblk)], axis=1)
                acc = alpha * acc + lax.dot_general(jnp.concatenate([vb, ones], axis=0), p, NN,
                                                    preferred_element_type=F32)
                new.append((m_new, acc))
            return tuple(new)

        def scores_into(buf, j0):
            for si, s in enumerate(scores(j0, KV_BLKS)):
                buf[si] = s

        def pair(it, carry):
            ja = n_ctx_blk + it * (2 * KV_BLKS)
            jb = ja + KV_BLKS
            jn = jnp.minimum(jb + KV_BLKS, n_all_blk - KV_BLKS)
            scores_into(sb_sc, jb)
            carry = update([sa_sc[si] for si in range(len(qms))], ja, carry, KV_BLKS)
            scores_into(sa_sc, jn)
            return update([sb_sc[si] for si in range(len(qms))], jb, carry, KV_BLKS)

        m0 = jnp.full((1, TM), -jnp.inf, F32)
        a0 = jnp.zeros((HD + V_ONES, TM), F32)
        scores_into(sa_sc, n_ctx_blk)
        res = lax.fori_loop(0, n_ctx_blk, lambda j, c: update(scores(j, 1), j, c, 1),
                            tuple((m0, a0) for _ in qms))
        res = lax.fori_loop(0, n_lat_pairs, pair, res)
        return [acc[0:HD] * (1.0 / acc[HD:HD + 1]) for (_, acc) in res]

    outs = []
    if diff:
        lv = lam_ref[...]
        lam = (jnp.exp(jnp.sum(lv[0:1] * lv[1:2], axis=1, keepdims=True))
               - jnp.exp(jnp.sum(lv[2:3] * lv[3:4], axis=1, keepdims=True)) + lam_init)
        for hp in range(2):
            qms, vlos = [], []
            for h in (2 * hp, 2 * hp + 1):
                for j in range(2):
                    lo = (2 * h + j) * (HD // 2)
                    qms.append(jnp.where((rowi >= lo) & (rowi < lo + HD // 2), qt, jnp.zeros_like(qt)))
                    vlos.append(h * HD)
            r = run_streams(qms, vlos)
            outs += [r[0] - lam * r[1], r[2] - lam * r[3]]
    else:
        qms, vlos = [], []
        for h in range(4):
            kvh = h // 2
            qh = qt[h * HD:(h + 1) * HD, :]
            z = jnp.zeros_like(qh)
            qms.append(jnp.concatenate([qh, z] if kvh == 0 else [z, qh], axis=0))
            vlos.append(kvh * HD)
        outs = run_streams(qms, vlos)
    o_ref[...] = jnp.concatenate(outs, axis=0).T


def _attn_call(qt, k, vt, lam, B, T, CT, diff, lam_init, name):
    nt = T // TM
    kd = k.shape[1]
    vr = vt.shape[2]
    KV_BLKS = KV_BLKS_DIFF if diff else KV_BLKS_GQA
    assert (nt - CT // TM) % (2 * KV_BLKS) == 0
    kern = functools.partial(_attn_kernel, diff=diff, lam_init=lam_init, n_ctx_tiles=CT // TM,
                             n_ctx_blk=CT // TM, n_all_blk=nt)
    score_buf = pltpu.VMEM((4, KV_BLKS * TM, TM), F32)
    return pl.pallas_call(
        kern, grid=(B, nt),
        in_specs=[pl.BlockSpec((1, 1, GW, TM), lambda b, i: (b, i, 0, 0)),
                  pl.BlockSpec((T, kd), lambda b, i: (b, 0)),
                  pl.BlockSpec((1, nt, vr, TM), lambda b, i: (b, 0, 0, 0)),
                  pl.BlockSpec(lam.shape, lambda b, i: (0, 0))],
        out_specs=pl.BlockSpec((TM, GW), lambda b, i: (b * nt + i, 0)),
        out_shape=jax.ShapeDtypeStruct((B * T, GW), F32),
        scratch_shapes=[score_buf, score_buf],
        compiler_params=_cparams(("arbitrary", "arbitrary")), name=name,
    )(qt, k, vt, lam)


def _chunk_masks(reverse):
    row = lax.broadcasted_iota(jnp.int32, (CH, GW), 0)
    lane = lax.broadcasted_iota(jnp.int32, (CH, GW), 1)
    sidx = lane % CH
    eye = row == sidx
    causal = (sidx >= row) if reverse else (sidx <= row)
    strict = (sidx > row) if reverse else (sidx < row)
    r2 = lax.broadcasted_iota(jnp.int32, (GW, GW), 0)
    c2 = lax.broadcasted_iota(jnp.int32, (GW, GW), 1)
    blk = (r2 // CH) == (c2 // CH)
    t1 = lax.broadcasted_iota(jnp.int32, (CH, CH), 0)
    t2 = lax.broadcasted_iota(jnp.int32, (CH, CH), 1)
    tri = ((t2 >= t1) if reverse else (t2 <= t1)).astype(F32)
    return eye, causal, strict, blk, tri


def _bd(y, blk):
    y16 = y.astype(BF16)
    return jnp.where(blk, jnp.concatenate([y16, y16, y16, y16], axis=0), jnp.zeros((), BF16))


def _dots_x3_bd(xs, y, blk):
    his, los = zip(*[_split(x) for x in xs])
    yh, yl = _split(y)
    bh, bl = _bd(yh, blk), _bd(yl, blk)
    d = functools.partial(lax.dot_general, dimension_numbers=NN, preferred_element_type=F32)
    n, k = xs[0].shape[0], len(xs)
    both = d(jnp.concatenate(his + los, axis=0), bh)
    hl = d(jnp.concatenate(his, axis=0), bl) if k > 1 else d(his[0], bl)
    return [both[i * n:(i + 1) * n] + both[(k + i) * n:(k + i + 1) * n] + hl[i * n:(i + 1) * n] for i in range(k)]


def _dot_x3_bd(x, y, blk):
    return _dots_x3_bd([x], y, blk)[0]


def _row_form(col, eye):
    return _dot_xr(jnp.ones((CH, CH), F32), jnp.where(eye, col, 0.0))


def _cummax(a, reverse):
    n = a.shape[0]
    row = lax.broadcasted_iota(jnp.int32, a.shape, 0)
    k = 1
    while k < n:
        if reverse:
            sh, valid = pltpu.roll(a, n - k, 0), row < n - k
        else:
            sh, valid = pltpu.roll(a, k, 0), row >= k
        a = jnp.maximum(a, jnp.where(valid, sh, -jnp.inf))
        k *= 2
    return a


def _chunk_order(p, nc_ctx, nch, reverse):
    if not reverse:
        return p
    return jnp.where(p < nc_ctx, nc_ctx - 1 - p, nch + nc_ctx - 1 - p)


def _mlstm_kernel(q_ref, k_ref, v_ref, g_ref, bias_ref, ei_ref, ef_ref, h_ref, c_sc, n_sc, m_sc, *, reverse):
    @pl.when(pl.program_id(0) == 0)
    def _():
        c_sc[...] = jnp.zeros_like(c_sc)
        n_sc[...] = jnp.zeros_like(n_sc)
        m_sc[...] = jnp.zeros_like(m_sc)

    eye, causal, _, blk, tri = _chunk_masks(reverse)
    last = 0 if reverse else CH - 1
    R = range(q_ref.shape[0])
    q4 = [q_ref[b] for b in R]
    k4 = [k_ref[b] * (HD ** -0.5) for b in R]
    v4 = [v_ref[b] for b in R]
    gt = [g_ref[b] + bias_ref[...] for b in R]
    i_e = [_dot_xl(gt[b], ei_ref[0]) for b in R]
    f_e = [_dot_xl(gt[b], ef_ref[0]) for b in R]
    logf = [jnp.minimum(f_e[b], 0.0) - jnp.log(1.0 + jnp.exp(-jnp.abs(f_e[b]))) for b in R]
    b_e = [_dot_xr(tri, logf[b]) for b in R]
    m_prev = [m_sc[b, 0:1, :] for b in R]
    a_e = [i_e[b] - b_e[b] for b in R]
    rowa = [_row_form(a_e[b], eye) for b in R]
    inter = [b_e[b] + m_prev[b] for b in R]
    m_t = [jnp.maximum(inter[b], b_e[b] + _cummax(a_e[b], reverse)) for b in R]
    w_inter = [jnp.exp(inter[b] - m_t[b]) for b in R]
    dexp = [jnp.exp(jnp.where(causal, b_e[b] - m_t[b] + rowa[b], -jnp.inf)) for b in R]
    smat = [_dot(q4[b], _bd(k4[b], blk), NT) * dexp[b] for b in R]
    cbd = [c_sc[b] for b in R]
    nbd = [n_sc[b] for b in R]
    qc = [_dot(q4[b], cbd[b]) for b in R]
    qn = [_dot(q4[b], nbd[b]) for b in R]
    sv = [_dot(smat[b], _bd(v4[b], blk)) for b in R]
    ssum = [_dot_xl(smat[b], blk.astype(F32)) for b in R]
    b_last = [b_e[b][last:last + 1, :] for b in R]
    g = [b_last[b] - b_e[b] + i_e[b] for b in R]
    m_new = [jnp.maximum(b_last[b] + m_prev[b], jnp.max(g[b], axis=0, keepdims=True)) for b in R]
    ks = [k4[b] * jnp.exp(g[b] - m_new[b]) for b in R]
    cup = [_dot(ks[b], v4[b], TN) for b in R]
    nup = [_dot(ks[b], jnp.ones((CH, GW), F32), TN) for b in R]
    for b in R:
        num = w_inter[b] * qc[b] + sv[b]
        den = w_inter[b] * qn[b] + ssum[b]
        h_ref[b] = num / jnp.maximum(jnp.abs(den), jnp.exp(-m_t[b]))
        a_prev = jnp.exp(b_last[b] + m_prev[b] - m_new[b])
        c_sc[b] = a_prev * cbd[b] + jnp.where(blk, cup[b], 0.0)
        n_sc[b] = a_prev * nbd[b] + jnp.where(blk, nup[b], 0.0)
        m_sc[b] = jnp.broadcast_to(m_new[b], (8, GW))


def _mlstm_call(U, bias128, e_i, e_f, B, T, CT, reverse):
    nch, ncc = T // CH, CT // CH
    U3 = U.reshape(B, T, U.shape[1])
    cmap = lambda col: (lambda p: (0, _chunk_order(p, ncc, nch, reverse), col))
    d = 1 if reverse else 0
    out = pl.pallas_call(
        functools.partial(_mlstm_kernel, reverse=reverse), grid=(nch,),
        in_specs=[pl.BlockSpec((B, CH, GW), cmap(U_CQ // GW)),
                  pl.BlockSpec((B, CH, GW), cmap(U_CK // GW)),
                  pl.BlockSpec((B, CH, GW), cmap(U_CV // GW)),
                  pl.BlockSpec((B, CH, LANES), cmap(U_G // LANES)),
                  pl.BlockSpec((1, LANES), lambda p: (0, 0)),
                  pl.BlockSpec((1, LANES, GW), lambda p: (d, 0, 0)),
                  pl.BlockSpec((1, LANES, GW), lambda p: (d, 0, 0))],
        out_specs=pl.BlockSpec((B, CH, GW), cmap(0)),
        out_shape=jax.ShapeDtypeStruct((B, T, GW), F32),
        scratch_shapes=[pltpu.VMEM((B, GW, GW), F32), pltpu.VMEM((B, GW, GW), F32), pltpu.VMEM((B, 8, GW), F32)],
        compiler_params=_cparams(("arbitrary",)),
        name="mlstm_bwd" if reverse else "mlstm_fwd",
    )(U3, U3, U3, U3, bias128, e_i, e_f)
    return out.reshape(B * T, GW)


def _gdn_prep_kernel(prev_ref, cur_ref, next_ref, g_ref, conv_ref, g64_ref, e_ref, alog_ref, dtb_ref,
                     q_ref, k_ref, v_ref, bf_ref, gf_ref, bb_ref, gb_ref, *, n_ctx_tiles, nt):
    i = pl.program_id(1)
    at_start = (i == 0) | (i == n_ctx_tiles)
    at_end = (i == n_ctx_tiles - 1) | (i == nt - 1)
    cur = cur_ref[...]
    head = jnp.where(at_start, 0.0, prev_ref[...])
    tail = jnp.where(at_end, 0.0, next_ref[...])
    xx = jnp.concatenate([head, cur, tail], axis=0)
    cw = conv_ref[...]
    kw = cw.shape[0]
    acc = jnp.zeros_like(cur)
    for j in range(kw):
        s0 = 8 + j - kw // 2
        acc = acc + xx[s0:s0 + TM, :] * cw[j:j + 1, :]
    t = _silu(acc)
    q, k, v = t[:, 0:GW], t[:, GW:2 * GW], t[:, 2 * GW:3 * GW]
    g64 = g64_ref[...] * float(HD)
    q_ref[...] = q * lax.rsqrt(_dot_xl(q * q, g64) + EPS) * (HD ** -0.5)
    k_ref[...] = k * lax.rsqrt(_dot_xl(k * k, g64) + EPS)
    v_ref[...] = v
    gt = g_ref[...]
    for d, (b_out, g_out) in enumerate(((bf_ref, gf_ref), (bb_ref, gb_ref))):
        b_out[...] = _sigmoid(_dot_xl(gt, e_ref[2 * d]))
        a_e = _dot_xl(gt, e_ref[2 * d + 1])
        g_out[...] = -jnp.exp(alog_ref[d:d + 1, :]) * _softplus(a_e + dtb_ref[d:d + 1, :])


def _gdn_prep_call(U, conv_w, g64, e_d, alog_e, dtb_e, B, T, CT):
    nt = T // TM
    w3 = 3 * GW
    r8 = TM // 8
    row = lambda b, i: (b * nt + i, 0)
    const2 = lambda b, i: (0, 0)
    o = jax.ShapeDtypeStruct((B * T, GW), F32)
    return pl.pallas_call(
        functools.partial(_gdn_prep_kernel, n_ctx_tiles=CT // TM, nt=nt), grid=(B, nt),
        in_specs=[pl.BlockSpec((8, w3), lambda b, i: (jnp.maximum((b * nt + i) * r8 - 1, 0), 0)),
                  pl.BlockSpec((TM, w3), row),
                  pl.BlockSpec((8, w3), lambda b, i: (jnp.minimum((b * nt + i + 1) * r8, B * nt * r8 - 1), 0)),
                  pl.BlockSpec((TM, LANES), lambda b, i: (b * nt + i, U_G // LANES)),
                  pl.BlockSpec(conv_w.shape, const2),
                  pl.BlockSpec((GW, GW), const2),
                  pl.BlockSpec(e_d.shape, lambda b, i: (0, 0, 0)),
                  pl.BlockSpec((2, GW), const2), pl.BlockSpec((2, GW), const2)],
        out_specs=[pl.BlockSpec((TM, GW), row)] * 7,
        out_shape=[o] * 7,
        compiler_params=_cparams(("arbitrary", "arbitrary")), name="gdn_prep",
    )(U, U, U, U, conv_w, g64, e_d, alog_e, dtb_e)


def _gdn_kernel(q_ref, k_ref, v_ref, beta_ref, g_ref, o_ref, s_sc, *, reverse):
    @pl.when(pl.program_id(0) == 0)
    def _():
        s_sc[...] = jnp.zeros_like(s_sc)

    eye, causal, strict, blk, tri = _chunk_masks(reverse)
    last = 0 if reverse else CH - 1
    R = range(q_ref.shape[0])
    q4, k4, v4, beta = ([r[b] for b in R] for r in (q_ref, k_ref, v_ref, beta_ref))
    gam = [_dot_xr(tri, g_ref[b]) for b in R]
    rowg = [_row_form(gam[b], eye) for b in R]
    decay = [jnp.exp(jnp.where(causal, gam[b] - rowg[b], -jnp.inf)) for b in R]
    kbeta = [k4[b] * beta[b] for b in R]
    kbd = [_bd(k4[b], blk) for b in R]
    a4 = [jnp.where(strict, _dot(kbeta[b], kbd[b], NT) * decay[b], 0.0) for b in R]
    attn = [_dot(q4[b], kbd[b], NT) * decay[b] for b in R]
    tinv = [jnp.where(eye, 1.0, 0.0) - a4[b] for b in R]
    apow = [_dot_x3_bd(a4[b], a4[b], blk) for b in R]
    for _ in range(4):
        both = [_dots_x3_bd([apow[b], tinv[b]], apow[b], blk) for b in R]
        tinv = [tinv[b] + both[b][1] for b in R]
        apow = [both[b][0] for b in R]
    tinv = [tinv[b] + _dot_x3_bd(tinv[b], apow[b], blk) for b in R]
    u4 = [_dot_x3_bd(tinv[b], v4[b] * beta[b], blk) for b in R]
    w4 = [_dot_x3_bd(tinv[b], kbeta[b] * jnp.exp(gam[b]), blk) for b in R]
    sbd = [s_sc[b] for b in R]
    v_new = [u4[b] - _dot(w4[b], sbd[b]) for b in R]
    o_inter = [_dot(q4[b] * jnp.exp(gam[b]), sbd[b]) for b in R]
    o_intra = [_dot(attn[b], _bd(v_new[b], blk)) for b in R]
    upd = [_dot(k4[b] * jnp.exp(gam[b][last:last + 1, :] - gam[b]), v_new[b], TN) for b in R]
    for b in R:
        o_ref[b] = o_inter[b] + o_intra[b]
        s_sc[b] = sbd[b] * jnp.exp(gam[b][last:last + 1, :]) + jnp.where(blk, upd[b], 0.0)


def _gdn_call(q, k, v, beta, g, B, T, CT, reverse):
    nch, ncc = T // CH, CT // CH
    spec = pl.BlockSpec((B, CH, GW), lambda p: (0, _chunk_order(p, ncc, nch, reverse), 0))
    r3 = lambda a: a.reshape(B, T, GW)
    out = pl.pallas_call(
        functools.partial(_gdn_kernel, reverse=reverse), grid=(nch,),
        in_specs=[spec] * 5, out_specs=spec,
        out_shape=jax.ShapeDtypeStruct((B, T, GW), F32),
        scratch_shapes=[pltpu.VMEM((B, GW, GW), F32)],
        compiler_params=_cparams(("arbitrary",)),
        name="gdn_bwd" if reverse else "gdn_fwd",
    )(r3(q), r3(k), r3(v), r3(beta), r3(g))
    return out.reshape(B * T, GW)


def _post_kernel(x_ref, mod0_ref, mod1_ref, oa_ref, ob_ref, hcf_ref, hcb_ref, odf_ref, odb_ref, oc_ref, zd_ref,
                 gains_ref, g64_ref, w_ref, xo_ref, *, lam_init):
    gains = gains_ref[...]
    g64 = g64_ref

    oa = oa_ref[...]
    ya = oa * lax.rsqrt(_group_mean(oa * oa, g64) + EPS) * gains[0:1] * (1.0 - lam_init)
    yb = ob_ref[...]
    hc = hcf_ref[...] + hcb_ref[...]
    dc = hc - _group_mean(hc, g64)
    yc = _sigmoid(oc_ref[...]) * (dc * lax.rsqrt(_group_mean(dc * dc, g64) + EPS) * gains[1:2])
    od = odf_ref[...] + odb_ref[...]
    yd = od * lax.rsqrt(_group_mean(od * od, g64) + EPS) * gains[2:3] * _silu(zd_ref[...])
    out = _dot(ya, w_ref[0:GW, :])
    out = out + _dot(yb, w_ref[GW:2 * GW, :])
    out = out + _dot(yc, w_ref[2 * GW:3 * GW, :])
    out = out + _dot(yd, w_ref[3 * GW:4 * GW, :])
    for t, m_ref in enumerate((mod0_ref, mod1_ref)):
        rows = slice(t * TM, (t + 1) * TM)
        xo_ref[rows, :] = x_ref[rows, :] + m_ref[0, 0][2:3] * out[rows]


def _post_call(X, modtab, oa, ob, hcf, hcb, odf, odb, U, gains, g64, w_out, B, T, lam_init):
    nt = T // TM
    d = X.shape[1]
    assert (B * nt) % 2 == 0
    row = lambda p: (p, 0)
    const = lambda p: (0, 0)
    mod_of = lambda half: (lambda p: ((2 * p + half) // nt, jnp.minimum((2 * p + half) % nt, 1), 0, 0))
    gspec = pl.BlockSpec((2 * TM, GW), row)
    return pl.pallas_call(
        functools.partial(_post_kernel, lam_init=lam_init), grid=(B * nt // 2,),
        in_specs=[pl.BlockSpec((2 * TM, d), row),
                  pl.BlockSpec((1, 1, 8, d), mod_of(0)), pl.BlockSpec((1, 1, 8, d), mod_of(1)),
                  gspec, gspec, gspec, gspec, gspec, gspec,
                  pl.BlockSpec((2 * TM, GW), lambda p: (p, U_CO // GW)),
                  pl.BlockSpec((2 * TM, GW), lambda p: (p, U_DZ // GW)),
                  pl.BlockSpec((8, GW), const), pl.BlockSpec((GW, GW), const),
                  pl.BlockSpec(w_out.shape, const)],
        out_specs=pl.BlockSpec((2 * TM, d), row),
        out_shape=jax.ShapeDtypeStruct(X.shape, F32),
        compiler_params=_cparams(("arbitrary",)), name="out_proj",
    )(X, modtab, modtab, oa, ob, hcf, hcb, odf, odb, U, U, gains, g64, w_out)


FF_CHUNK = 256


def _ffn_kernel(x_ref, mod0_ref, mod1_ref, gain_ref, wg_ref, wu_ref, wd_ref, xo_ref):
    x = x_ref[...]
    mods = (mod0_ref[0, 0], mod1_ref[0, 0])
    h = jnp.concatenate([_norm_mod(x[t * TM:(t + 1) * TM], gain_ref[...], m[3:4], m[4:5])
                         for t, m in enumerate(mods)], axis=0).astype(BF16)
    acc = jnp.zeros(x.shape, F32)
    for f in range(0, wg_ref.shape[1], FF_CHUNK):
        g = lax.dot_general(h, wg_ref[:, f:f + FF_CHUNK], NN, preferred_element_type=F32)
        u = lax.dot_general(h, wu_ref[:, f:f + FF_CHUNK], NN, preferred_element_type=F32)
        acc = acc + _dot(_silu(g) * u, wd_ref[f:f + FF_CHUNK, :])
    for t, m in enumerate(mods):
        xo_ref[t * TM:(t + 1) * TM, :] = x[t * TM:(t + 1) * TM] + m[5:6] * acc[t * TM:(t + 1) * TM]


def _ffn_call(X, modtab, gain, wg, wu, wd, nb, tiles_per_b, kind_of_tile):
    d = X.shape[1]
    n_tiles = nb * tiles_per_b
    assert n_tiles % 2 == 0
    const = lambda p: (0, 0)
    mod_of = lambda half: (lambda p: ((2 * p + half) // tiles_per_b, kind_of_tile((2 * p + half) % tiles_per_b), 0, 0))
    return pl.pallas_call(
        _ffn_kernel, grid=(n_tiles // 2,),
        in_specs=[pl.BlockSpec((2 * TM, d), lambda p: (p, 0)),
                  pl.BlockSpec((1, 1, 8, d), mod_of(0)),
                  pl.BlockSpec((1, 1, 8, d), mod_of(1)),
                  pl.BlockSpec((1, d), const),
                  pl.BlockSpec(wg.shape, const), pl.BlockSpec(wu.shape, const), pl.BlockSpec(wd.shape, const)],
        out_specs=pl.BlockSpec((2 * TM, d), lambda p: (p, 0)),
        out_shape=jax.ShapeDtypeStruct(X.shape, F32),
        compiler_params=_cparams(("arbitrary",)), name="ffn_dense",
    )(X, modtab, modtab, gain, wg, wu, wd)


def _router_kernel(x_ref, mod_ref, gain_ref, r_ref, h_ref, gates_ref, *, n_exp):
    x = x_ref[...]
    mod = mod_ref[0, 0]
    h = _norm_mod(x, gain_ref[...], mod[3:4], mod[4:5])
    h_ref[...] = h.astype(BF16)
    logits = _dot_x3(h, r_ref[...])
    lane = lax.broadcasted_iota(jnp.int32, logits.shape, 1)
    neg = -jnp.inf
    lg = jnp.where(lane < n_exp, logits, neg)
    v1 = jnp.max(lg, axis=1, keepdims=True)
    i1 = jnp.min(jnp.where(lg == v1, lane, LANES), axis=1, keepdims=True)
    lg2 = jnp.where(lane == i1, neg, lg)
    v2 = jnp.max(lg2, axis=1, keepdims=True)
    i2 = jnp.min(jnp.where(lg2 == v2, lane, LANES), axis=1, keepdims=True)
    e2 = jnp.exp(v2 - v1)
    w1 = 1.0 / (1.0 + e2)
    w2 = e2 / (1.0 + e2)
    gates_ref[...] = jnp.where(lane == i1, w1, 0.0) + jnp.where(lane == i2, w2, 0.0)


def _router_call(X, modtab, gain, router128, nb, tiles_per_b, n_exp):
    d = X.shape[1]
    row = lambda b, i: (b * tiles_per_b + i, 0)
    const = lambda b, i: (0, 0)
    return pl.pallas_call(
        functools.partial(_router_kernel, n_exp=n_exp), grid=(nb, tiles_per_b),
        in_specs=[pl.BlockSpec((TM, d), row),
                  pl.BlockSpec((1, 1, 8, d), lambda b, i: (b, 1, 0, 0)),
                  pl.BlockSpec((1, d), const),
                  pl.BlockSpec(router128.shape, const)],
        out_specs=[pl.BlockSpec((TM, d), row), pl.BlockSpec((TM, LANES), row)],
        out_shape=[jax.ShapeDtypeStruct(X.shape, BF16), jax.ShapeDtypeStruct((X.shape[0], LANES), F32)],
        compiler_params=_cparams(("arbitrary", "arbitrary")), name="moe_router",
    )(X, modtab, gain, router128)


MOE_TM = 1024
MOE_C0 = 640
MOE_CG = 128
MOE_CH = 256
MOE_TF = 896


def _moe_kernel(x_ref, mod_ref, h_ref, gates_ref, tri_ref, wg_ref, wu_ref, wd_ref, xo_ref,
                acc_sc, rk_sc, rkt_sc, xg0_sc, y0_sc, xg_sc, y_sc, nck_sm):
    e = pl.program_id(2)
    f = pl.program_id(3)
    ne = pl.num_programs(2)
    nf = pl.num_programs(3)
    tm = h_ref.shape[0]
    c0 = xg0_sc.shape[0]
    sizes = tuple(range(MOE_CG, c0 + 1, MOE_CG))

    def for_block0_size(fn):
        for idx, rows in enumerate(sizes):
            pl.when(nck_sm[0] == idx)(functools.partial(fn, rows))

    def onehot(base, rows, rrow):
        r = lax.broadcasted_iota(jnp.int32, (rows, 1), 0).astype(F32) + base
        return jnp.where(r == rrow, 1.0, 0.0).astype(BF16)

    def onehot_t(base, rows, rcol):
        c = lax.broadcasted_iota(jnp.int32, (1, rows), 1).astype(F32) + base
        return jnp.where(rcol == c, 1.0, 0.0).astype(BF16)

    def ffn(xk):
        g = lax.dot_general(xk, wg_ref[0], NN, preferred_element_type=F32)
        u = lax.dot_general(xk, wu_ref[0], NN, preferred_element_type=F32)
        return _dot(_silu(g) * u, wd_ref[0])

    def rest_base(k):
        return (c0 + k * MOE_CH).astype(F32)

    @pl.when((e == 0) & (f == 0))
    def _():
        acc_sc[...] = jnp.zeros_like(acc_sc)
        sel = gates_ref[...] > 0.0
        rank = lax.dot_general(tri_ref[...], jnp.where(sel, 1.0, 0.0).astype(BF16), NN,
                               preferred_element_type=F32)
        rk = jnp.where(sel, rank, -1.0)
        rk_sc[...] = rk
        rkt_sc[...] = rk.T

    @pl.when(f == 0)
    def _():
        rrow = rkt_sc[pl.ds(e, 1), :]
        cnt = jnp.sum(jnp.where(rrow >= 0.0, 1.0, 0.0))
        icnt = cnt.astype(jnp.int32)
        nck_sm[0] = jnp.clip((icnt + (MOE_CG - 1)) // MOE_CG, 1, len(sizes)) - 1
        nck_sm[1] = (jnp.maximum(icnt - c0, 0) + (MOE_CH - 1)) // MOE_CH
        h = h_ref[...]

        def gather0(rows):
            xg0_sc[0:rows] = lax.dot_general(onehot(0.0, rows, rrow), h, NN,
                                             preferred_element_type=F32).astype(BF16)
            y0_sc[0:rows] = jnp.zeros((rows, y0_sc.shape[1]), F32)

        for_block0_size(gather0)

        def gather(k, carry):
            xg_sc[k] = lax.dot_general(onehot(rest_base(k), MOE_CH, rrow), h, NN,
                                       preferred_element_type=F32).astype(BF16)
            y_sc[k] = jnp.zeros((MOE_CH, y_sc.shape[2]), F32)
            return carry

        lax.fori_loop(0, nck_sm[1], gather, 0)

    def expert0(rows):
        y0_sc[0:rows] += ffn(xg0_sc[0:rows])

    for_block0_size(expert0)

    def expert(k, carry):
        y_sc[k] += ffn(xg_sc[k])
        return carry

    lax.fori_loop(0, nck_sm[1], expert, 0)

    @pl.when(f == nf - 1)
    def _():
        lane = lax.broadcasted_iota(jnp.int32, (tm, LANES), 1)
        rcol = jnp.sum(jnp.where(lane == e, rk_sc[...], 0.0), axis=1, keepdims=True)
        gcol = jnp.sum(jnp.where(lane == e, gates_ref[...], 0.0), axis=1, keepdims=True)

        def scatter0(rows):
            acc_sc[...] += gcol * lax.dot_general(onehot_t(0.0, rows, rcol), y0_sc[0:rows].astype(BF16), NN,
                                                  preferred_element_type=F32)

        for_block0_size(scatter0)

        def scatter(k, carry):
            acc_sc[...] += gcol * lax.dot_general(onehot_t(rest_base(k), MOE_CH, rcol), y_sc[k].astype(BF16), NN,
                                                  preferred_element_type=F32)
            return carry

        lax.fori_loop(0, nck_sm[1], scatter, 0)

    @pl.when((e == ne - 1) & (f == nf - 1))
    def _():
        xo_ref[...] = x_ref[...] + mod_ref[0, 0][5:6] * acc_sc[...]


def _moe_call(X, modtab, hb, gates, wg, wu, wd, nb, rows_per_b):
    d = X.shape[1]
    n_exp, _, ff = wg.shape
    tm = min(MOE_TM, rows_per_b)
    tpb = rows_per_b // tm
    c0 = min(MOE_C0, tm)
    nrest = max(-(-(tm - c0) // MOE_CH), 1)
    tri = jnp.asarray(np.tril(np.ones((tm, tm), np.float32), -1), BF16)
    row = lambda b, i, e, f: (b * tpb + i, 0)
    return pl.pallas_call(
        _moe_kernel, grid=(nb, tpb, n_exp, ff // MOE_TF),
        in_specs=[pl.BlockSpec((tm, d), row),
                  pl.BlockSpec((1, 1, 8, d), lambda b, i, e, f: (b, 1, 0, 0)),
                  pl.BlockSpec((tm, d), row),
                  pl.BlockSpec((tm, LANES), row),
                  pl.BlockSpec((tm, tm), lambda b, i, e, f: (0, 0)),
                  pl.BlockSpec((1, d, MOE_TF), lambda b, i, e, f: (e, 0, f)),
                  pl.BlockSpec((1, d, MOE_TF), lambda b, i, e, f: (e, 0, f)),
                  pl.BlockSpec((1, MOE_TF, d), lambda b, i, e, f: (e, f, 0))],
        out_specs=pl.BlockSpec((tm, d), row),
        out_shape=jax.ShapeDtypeStruct(X.shape, F32),
        scratch_shapes=[pltpu.VMEM((tm, d), F32), pltpu.VMEM((tm, LANES), F32), pltpu.VMEM((LANES, tm), F32),
                        pltpu.VMEM((c0, d), BF16), pltpu.VMEM((c0, d), F32),
                        pltpu.VMEM((nrest, MOE_CH, d), BF16), pltpu.VMEM((nrest, MOE_CH, d), F32),
                        pltpu.SMEM((2,), jnp.int32)],
        compiler_params=_cparams(("arbitrary",) * 4), name="moe_experts",
    )(X, modtab, hb, gates, tri, wg, wu, wd)


def _group_mean_matrix(group):
    idx = np.arange(GW)
    return jnp.asarray((idx[:, None] // group == idx[None, :] // group).astype(np.float32) / group, BF16)


def _rope_tables(n_lat, n_ctx, dim, reps):
    t = jnp.arange(n_lat)
    pos = jnp.stack([(t // GRID_W).astype(F32), (t % GRID_W).astype(F32)], axis=1)
    n_freq = dim // 4
    inv_freq = ROPE_THETA ** (-jnp.arange(n_freq, dtype=F32) / n_freq)
    ang = pos[:, :, None] * inv_freq
    cos = jnp.cos(ang)
    sin = jnp.sin(ang)
    cos = jnp.stack([cos, cos], axis=2).reshape(n_lat, dim)
    sin = jnp.stack([-sin, sin], axis=2).reshape(n_lat, dim)
    cos = jnp.concatenate([jnp.ones((n_ctx, dim), F32), cos], axis=0)
    sin = jnp.concatenate([jnp.zeros((n_ctx, dim), F32), sin], axis=0)
    return jnp.tile(cos, (1, reps)), jnp.tile(sin, (1, reps))


def _expand_matrix(col0):
    m = np.zeros((LANES, GW), np.float32)
    for h in range(4):
        m[col0 + h, h * HD:(h + 1) * HD] = 1.0
    return m


_SPLITS = (256, 256, 256, 256, 128, 128, 256, 256, 256, 256, 16, 768, 256, 16)


def _reorder_w_in(w):
    offs = np.concatenate([[0], np.cumsum(_SPLITS)])
    seg = [w[:, offs[i]:offs[i + 1]] for i in range(len(_SPLITS))]
    aq, ak, av, bq, bk, bv, cq, ck, cv, co, cg, dqkv, dz, dg = seg
    pad = jnp.zeros((w.shape[0], LANES - 32), w.dtype)
    return jnp.concatenate([aq, ak, av, bq, bk, bv, dqkv, cq, ck, cv, co, dz, cg, dg, pad], axis=1).astype(BF16)


def kernel(x, c, ctx, c_ctx, w_mod, b_mod, norm1, norm2, w_in, w_out, diff_qk_gain, diff_lambda, diff_subln,
           gqa_qk_gain, mlstm_gate_bias, mlstm_norm, gdn_conv, gdn_a_log, gdn_dt_bias, gdn_norm,
           ffn_w_gate, ffn_w_up, ffn_w_down, moe_router, moe_w_gate, moe_w_up, moe_w_down):
    B, S, D = x.shape
    CT = ctx.shape[1]
    T = CT + S
    depth = w_in.shape[0]
    assert CT % TM == 0 and S % TM == 0 and S % GRID_W == 0 and B + 1 <= 8

    g32 = _group_mean_matrix(HD // 2)
    g64 = _group_mean_matrix(HD)
    cosa, sina = _rope_tables(S, CT, HD // 2, 8)
    cosb, sinb = _rope_tables(S, CT, HD, 4)
    e_i = jnp.asarray(np.stack([_expand_matrix(0), _expand_matrix(8)]), F32)
    e_f = jnp.asarray(np.stack([_expand_matrix(4), _expand_matrix(12)]), F32)
    e_d = jnp.asarray(np.stack([_expand_matrix(16 + 4 * j) for j in range(4)]), F32)

    cvec = jnp.zeros((8, D), F32).at[:B].set(c).at[B].set(c_ctx)
    mod_all = _mod_call(cvec, w_mod, b_mod).reshape(depth, 8, 6, D)

    X = jnp.concatenate([ctx, x], axis=1).reshape(B * T, D)
    nb_rows = T
    for li in range(depth):
        with_ctx = li < depth - 1
        lam_init = 0.8 - 0.6 * math.exp(-0.3 * li)
        m = mod_all[li]
        modtab = jnp.stack([jnp.broadcast_to(m[B], (B, 6, D)), m[:B]], axis=1)
        modtab = jnp.pad(modtab, ((0, 0), (0, 0), (0, 2), (0, 0)))

        qkg = jnp.zeros((8, GW), F32)
        qkg = qkg.at[0].set(jnp.tile(diff_qk_gain[li, 0], 8)).at[1].set(jnp.tile(diff_qk_gain[li, 1], 8))
        qkg = qkg.at[2].set(jnp.tile(gqa_qk_gain[li, 0], 4)).at[3].set(jnp.tile(gqa_qk_gain[li, 1], 4))
        qta, ka, vta, qtb, kb, vtb, U = _kin_call(
            X, modtab, norm1[li][None, :], _reorder_w_in(w_in[li]), g32, g64, qkg, cosa, sina, cosb, sinb, B, T)

        oa = _attn_call(qta, ka, vta, diff_lambda[li], B, T, CT, True, lam_init, "attn_diff")
        ob = _attn_call(qtb, kb, vtb, diff_lambda[li], B, T, CT, False, lam_init, "attn_gqa")

        bias128 = jnp.zeros((1, LANES), F32).at[0, :16].set(mlstm_gate_bias[li].reshape(-1))
        hcf = _mlstm_call(U, bias128, e_i, e_f, B, T, CT, False)
        hcb = _mlstm_call(U, bias128, e_i, e_f, B, T, CT, True)

        alog_e = jnp.repeat(gdn_a_log[li], HD, axis=1)
        dtb_e = jnp.repeat(gdn_dt_bias[li], HD, axis=1)
        qd, kd, vd, bf, gf, bb, gb = _gdn_prep_call(U, gdn_conv[li], g64, e_d, alog_e, dtb_e, B, T, CT)
        odf = _gdn_call(qd, kd, vd, bf, gf, B, T, CT, False)
        odb = _gdn_call(qd, kd, vd, bb, gb, B, T, CT, True)

        gains = jnp.zeros((8, GW), F32)
        gains = gains.at[0].set(jnp.tile(diff_subln[li], 4)).at[1].set(mlstm_norm[li])
        gains = gains.at[2].set(jnp.tile(gdn_norm[li], 4))
        X = _post_call(X, modtab, oa, ob, hcf, hcb, odf, odb, U, gains, g64, w_out[li].astype(BF16), B, T, lam_init)

        if with_ctx:
            tiles_per_b, kind = T // TM, (lambda i: jnp.minimum(i, 1))
        else:
            X = X.reshape(B, T, D)[:, CT:].reshape(B * S, D)
            nb_rows = S
            tiles_per_b, kind = S // TM, (lambda i: 1)
        j = li // 2
        if li % 2 == 0:
            X = _ffn_call(X, modtab, norm2[li][None, :], ffn_w_gate[j].astype(BF16), ffn_w_up[j].astype(BF16),
                          ffn_w_down[j].astype(BF16), B, tiles_per_b, kind)
        else:
            assert not with_ctx
            n_exp = moe_router.shape[2]
            router128 = jnp.pad(moe_router[j], ((0, 0), (0, LANES - n_exp)))
            hb, gates = _router_call(X, modtab, norm2[li][None, :], router128, B, tiles_per_b, n_exp)
            X = _moe_call(X, modtab, hb, gates, moe_w_gate[j].astype(BF16), moe_w_up[j].astype(BF16),
                          moe_w_down[j].astype(BF16), B, nb_rows)
    return X.reshape(B, nb_rows, D)[:, nb_rows - S:]
```
